```python
import jax, jax.numpy as jnp
from jax import lax
import numpy as np

D_MODEL = 1024
BATCH = 2
SEQ = 8192
DEPTH = 1

PLE_DIM = 256
HEAD_DIM = 64
FOX_HEADS = 8
SB_HEADS = 8
FOX_WIDTH = FOX_HEADS * HEAD_DIM
SB_WIDTH = SB_HEADS * HEAD_DIM
D_FF = 2816
Q_BLOCK = 128
EPS = 1e-6
FORGET_BIAS_INIT = 2.0
IN_SIZES = (FOX_WIDTH, FOX_WIDTH, FOX_WIDTH, FOX_HEADS, SB_WIDTH, SB_WIDTH, SB_WIDTH, D_MODEL, D_MODEL)
IN_WIDTH = 3 * FOX_WIDTH + FOX_HEADS + 3 * SB_WIDTH + 2 * D_MODEL

kernel_name = "hybrid_fox_stickbreak_macaron_ple"


def rms_norm(x, g):
    xf = x.astype(jnp.float32)
    y = xf * lax.rsqrt(jnp.mean(xf * xf, axis=-1, keepdims=True) + EPS)
    return (y * g.astype(jnp.float32)).astype(x.dtype)


def swiglu(h, w_gate, w_up, w_down):
    return (jax.nn.silu(h @ w_gate) * (h @ w_up)) @ w_down


def to_blocks(t):
    b, s = t.shape[0], t.shape[1]
    return jnp.moveaxis(t.reshape(b, s // Q_BLOCK, Q_BLOCK, *t.shape[2:]), 1, 0)


def from_blocks(t):
    nb, b = t.shape[0], t.shape[1]
    return jnp.moveaxis(t, 0, 1).reshape(b, nb * Q_BLOCK, -1)


def forgetting_attention(q, k, v, log_f):
    s_len = q.shape[1]
    scale = HEAD_DIM ** -0.5
    F = jnp.cumsum(log_f, axis=1)
    Fk = jnp.transpose(F, (0, 2, 1))[:, :, None, :]
    kpos = jnp.arange(s_len)
    qpos = kpos.reshape(s_len // Q_BLOCK, Q_BLOCK)

    def one_block(args):
        qi, Fi, pi = args
        logits = jnp.einsum('bqhd,bkhd->bhqk', qi, k).astype(jnp.float32) * scale
        logits = logits + jnp.transpose(Fi, (0, 2, 1))[..., None] - Fk
        mask = pi[:, None] >= kpos[None, :]
        logits = jnp.where(mask, logits, -jnp.inf)
        w = jax.nn.softmax(logits, axis=-1)
        return jnp.einsum('bhqk,bkhd->bqhd', w.astype(v.dtype), v)

    out = lax.map(one_block, (to_blocks(q), to_blocks(F), qpos))
    return from_blocks(out)


def stick_breaking_attention(q, k, v):
    s_len = q.shape[1]
    scale = HEAD_DIM ** -0.5
    kpos = jnp.arange(s_len)
    qpos = kpos.reshape(s_len // Q_BLOCK, Q_BLOCK)

    def one_block(args):
        qi, pi = args
        z = jnp.einsum('bqhd,bkhd->bhqk', qi, k).astype(jnp.float32) * scale
        mask = kpos[None, :] < pi[:, None]
        log_beta = jax.nn.log_sigmoid(z)
        log_1m = jnp.where(mask, jax.nn.log_sigmoid(-z), 0.0)
        after = lax.cumsum(log_1m, axis=3, reverse=True) - log_1m
        a = jnp.where(mask, jnp.exp(log_beta + after), 0.0)
        return jnp.einsum('bhqk,bkhd->bqhd', a.astype(v.dtype), v)

    out = lax.map(one_block, (to_blocks(q), qpos))
    return from_blocks(out)


def split_columns(t):
    outs, start = [], 0
    for size in IN_SIZES:
        outs.append(t[..., start:start + size])
        start += size
    return outs


def setup_inputs(seed: int = 0) -> dict:
    key = jax.random.key(seed)
    ks = jax.random.split(key, 24)

    def w(k, shape, fan_in):
        return jax.random.normal(k, shape, jnp.float32) * (fan_in ** -0.5)

    def gain(k, shape):
        return 1.0 + 0.05 * jax.random.normal(k, shape, jnp.float32)

    L = DEPTH
    return {
        "x": jax.random.normal(ks[0], (BATCH, SEQ, D_MODEL), jnp.float32),
        "p": jax.random.normal(ks[1], (DEPTH, BATCH, SEQ, PLE_DIM), jnp.float32),
        "ffn1_norm": gain(ks[2], (L, D_MODEL)),
        "ffn1_w_gate": w(ks[3], (L, D_MODEL, D_FF), D_MODEL),
        "ffn1_w_up": w(ks[4], (L, D_MODEL, D_FF), D_MODEL),
        "ffn1_w_down": w(ks[5], (L, D_FF, D_MODEL), D_FF),
        "mix_norm": gain(ks[6], (L, D_MODEL)),
        "w_in": w(ks[7], (L, D_MODEL, IN_WIDTH), D_MODEL),
        "forget_bias": FORGET_BIAS_INIT + 0.1 * jax.random.normal(ks[8], (L, FOX_HEADS), jnp.float32),
        "q_norm": gain(ks[9], (L, HEAD_DIM)),
        "k_norm": gain(ks[10], (L, HEAD_DIM)),
        "w_branch_fox": w(ks[11], (L, FOX_WIDTH, D_MODEL), FOX_WIDTH),
        "w_branch_sb": w(ks[12], (L, SB_WIDTH, D_MODEL), SB_WIDTH),
        "w_out": w(ks[13], (L, D_MODEL, D_MODEL), D_MODEL),
        "ffn2_norm": gain(ks[14], (L, D_MODEL)),
        "ffn2_w_gate": w(ks[15], (L, D_MODEL, D_FF), D_MODEL),
        "ffn2_w_up": w(ks[16], (L, D_MODEL, D_FF), D_MODEL),
        "ffn2_w_down": w(ks[17], (L, D_FF, D_MODEL), D_FF),
        "ple_norm": gain(ks[18], (L, D_MODEL)),
        "w_ple_gate": w(ks[19], (L, D_MODEL, D_MODEL), D_MODEL),
        "w_ple_proj": w(ks[20], (L, PLE_DIM, D_MODEL), PLE_DIM),
    }


def reference(x, p, ffn1_norm, ffn1_w_gate, ffn1_w_up, ffn1_w_down, mix_norm, w_in,
              forget_bias, q_norm, k_norm, w_branch_fox, w_branch_sb, w_out,
              ffn2_norm, ffn2_w_gate, ffn2_w_up, ffn2_w_down, ple_norm,
              w_ple_gate, w_ple_proj):
    b, s_len, _ = x.shape
    for i in range(DEPTH):
        x = x + 0.5 * swiglu(rms_norm(x, ffn1_norm[i]), ffn1_w_gate[i], ffn1_w_up[i], ffn1_w_down[i])

        h = rms_norm(x, mix_norm[i])
        fq, fk, fv, f_logit, sq, sk, sv, g_fox, g_sb = split_columns(h @ w_in[i])
        fq = rms_norm(fq.reshape(b, s_len, FOX_HEADS, HEAD_DIM), q_norm[i])
        fk = rms_norm(fk.reshape(b, s_len, FOX_HEADS, HEAD_DIM), k_norm[i])
        fv = fv.reshape(b, s_len, FOX_HEADS, HEAD_DIM)
        log_f = jax.nn.log_sigmoid((f_logit + forget_bias[i]).astype(jnp.float32))
        y_fox = forgetting_attention(fq, fk, fv, log_f)

        sq = sq.reshape(b, s_len, SB_HEADS, HEAD_DIM)
        sk = sk.reshape(b, s_len, SB_HEADS, HEAD_DIM)
        sv = sv.reshape(b, s_len, SB_HEADS, HEAD_DIM)
        y_sb = stick_breaking_attention(sq, sk, sv)

        merged = (jax.nn.sigmoid(g_fox) * (y_fox @ w_branch_fox[i])
                  + jax.nn.sigmoid(g_sb) * (y_sb @ w_branch_sb[i]))
        x = x + merged @ w_out[i]

        x = x + 0.5 * swiglu(rms_norm(x, ffn2_norm[i]), ffn2_w_gate[i], ffn2_w_up[i], ffn2_w_down[i])

        x = x + jax.nn.sigmoid(rms_norm(x, ple_norm[i]) @ w_ple_gate[i]) * (p[i] @ w_ple_proj[i])
    return x
```

```python
import functools

import jax
import jax.numpy as jnp
import numpy as np
from jax import lax
from jax.experimental import pallas as pl
from jax.experimental.pallas import tpu as pltpu

F32 = jnp.float32
BF16 = jnp.bfloat16

EPS = 1e-6
HEAD_DIM = 64
LOG2E = 1.4426950408889634
LANES = 128
VMEM_LIMIT_BYTES = 56 * 1024 * 1024

TOKEN_TILE = 512
ATTN_TILE = 256

NT_DIMS = (((1,), (1,)), ((), ()))


def _dot(a, b):
    return jnp.dot(a, b, preferred_element_type=F32)


def _split2(a):
    hi = a.astype(BF16)
    lo = (a - hi.astype(F32)).astype(BF16)
    return hi, lo


def _split3(a):
    p1 = a.astype(BF16)
    r1 = a - p1.astype(F32)
    p2 = r1.astype(BF16)
    r2 = r1 - p2.astype(F32)
    return p1, p2, r2.astype(BF16)


def _rms_rows(x, g):
    ms = jnp.mean(x * x, axis=-1, keepdims=True)
    return x * lax.rsqrt(ms + EPS) * g


def _sigmoid(x):
    return 1.0 / (1.0 + jnp.exp(-x))


def _params(*sem):
    return pltpu.CompilerParams(dimension_semantics=sem, vmem_limit_bytes=VMEM_LIMIT_BYTES)


def _const_spec(shape):
    nd = len(shape)
    return pl.BlockSpec(shape, lambda *_: (0,) * nd)


def _ffn_kernel(x_ref, g_ref, wg_ref, wu_ref, wd_ref, o_ref, h_scr, acc_scr):
    j = pl.program_id(1)

    @pl.when(j == 0)
    def _():
        h_scr[...] = _rms_rows(x_ref[...], g_ref[...]).astype(BF16)
        acc_scr[...] = jnp.zeros_like(acc_scr)

    h = h_scr[...]
    a = _dot(h, wg_ref[...])
    u = _dot(h, wu_ref[...])
    act = (a * _sigmoid(a) * u).astype(BF16)
    acc_scr[...] += _dot(act, wd_ref[...])

    @pl.when(j == pl.num_programs(1) - 1)
    def _():
        o_ref[...] = x_ref[...] + 0.5 * acc_scr[...]


def _ffn(x2d, g, wg, wu, wd):
    n, d = x2d.shape
    dff = wg.shape[1]
    tf = dff // 2
    tm = TOKEN_TILE
    return pl.pallas_call(
        _ffn_kernel,
        grid=(n // tm, dff // tf),
        in_specs=[
            pl.BlockSpec((tm, d), lambda i, j: (i, 0)),
            pl.BlockSpec((1, d), lambda i, j: (0, 0)),
            pl.BlockSpec((d, tf), lambda i, j: (0, j)),
            pl.BlockSpec((d, tf), lambda i, j: (0, j)),
            pl.BlockSpec((tf, d), lambda i, j: (j, 0)),
        ],
        out_specs=pl.BlockSpec((tm, d), lambda i, j: (i, 0)),
        out_shape=jax.ShapeDtypeStruct((n, d), F32),
        scratch_shapes=[pltpu.VMEM((tm, d), BF16), pltpu.VMEM((tm, d), F32)],
        compiler_params=_params("parallel", "arbitrary"),
        name="ffn",
    )(x2d, g, wg, wu, wd)


def _inproj_kernel(x_ref, g_ref, wq_ref, wk_ref, wv_ref, wsb_ref, wgate_ref, wf_ref,
                   fbias_ref, gq_ref, gk_ref, sel_ref, selt_ref, tri_ref,
                   pq_ref, pk_ref, cq_ref, ck_ref,
                   qf_ref, kf_ref, fv_ref, sq_ref, sk_ref, sv_ref, gf_ref, gs_ref,
                   carry_scr):
    @pl.when(pl.program_id(1) == 0)
    def _():
        carry_scr[...] = jnp.zeros_like(carry_scr)

    h = _rms_rows(x_ref[0], g_ref[...]).astype(BF16)

    yf = _dot(h, wf_ref[...]) + fbias_ref[...]
    lf = jnp.minimum(yf, 0.0) - jnp.log1p(jnp.exp(-jnp.abs(yf)))
    lane = lax.broadcasted_iota(jnp.int32, lf.shape, 1)
    n_heads = qf_ref.shape[2] // LANES
    lf = jnp.where(lane < n_heads, lf, 0.0)
    c = _dot(tri_ref[...], jnp.concatenate(_split3(lf), axis=1))
    f_cum = c[:, :LANES] + c[:, LANES:2 * LANES] + c[:, 2 * LANES:] + carry_scr[...]
    carry_scr[...] = f_cum[-1:, :]
    fparts = jnp.concatenate(_split3(f_cum * LOG2E), axis=1)

    def headnorm(y, gain):
        hi, lo = _split2(y * y)
        ms = _dot(hi, sel_ref[...]) + _dot(lo, sel_ref[...])
        rhi, rlo = _split2(lax.rsqrt(ms + EPS))
        return y * (_dot(rhi, selt_ref[...]) + _dot(rlo, selt_ref[...])) * gain

    q = headnorm(_dot(h, wq_ref[...]), gq_ref[...])
    qf_ref[0] = (q + _dot(fparts, pq_ref[...]) + cq_ref[...]).astype(BF16)
    k = headnorm(_dot(h, wk_ref[...]), gk_ref[...])
    kf_ref[0] = (k + _dot(fparts, pk_ref[...]) + ck_ref[...]).astype(BF16)
    fv_ref[0] = _dot(h, wv_ref[...]).astype(BF16)

    width = sq_ref.shape[2]
    ysb = _dot(h, wsb_ref[...])
    sq_ref[0] = (ysb[:, :width] * (LOG2E * HEAD_DIM ** -0.5)).astype(BF16)
    sk_ref[0] = ysb[:, width:2 * width].astype(BF16)
    sv_ref[0] = ysb[:, 2 * width:].astype(BF16)

    d = gf_ref.shape[2]
    yg = _sigmoid(_dot(h, wgate_ref[...]))
    gf_ref[0] = yg[:, :d].astype(BF16)
    gs_ref[0] = yg[:, d:].astype(BF16)


def _inproj(x3d, consts):
    b, s, d = x3d.shape
    tm = TOKEN_TILE
    spread = consts[1].shape[1]
    width = consts[3].shape[1]
    ins = [x3d] + list(consts)
    in_specs = [pl.BlockSpec((1, tm, d), lambda bi, i: (bi, i, 0))]
    in_specs += [_const_spec(c.shape) for c in consts]

    def tok(n):
        return pl.BlockSpec((1, tm, n), lambda bi, i: (bi, i, 0))

    def shp(n):
        return jax.ShapeDtypeStruct((b, s, n), BF16)

    return pl.pallas_call(
        _inproj_kernel,
        grid=(b, s // tm),
        in_specs=in_specs,
        out_specs=[tok(spread), tok(spread), tok(width), tok(width), tok(width), tok(width),
                   tok(d), tok(d)],
        out_shape=[shp(spread), shp(spread), shp(width), shp(width), shp(width), shp(width),
                   shp(d), shp(d)],
        scratch_shapes=[pltpu.VMEM((1, LANES), F32)],
        compiler_params=_params("parallel", "arbitrary"),
        name="inproj",
    )(*ins)


def _fox_kernel(qf_ref, kf_ref, v_ref, o_ref, m_scr, l_scr, acc_scr):
    i = pl.program_id(2)
    t = qf_ref.shape[1]
    row = lax.broadcasted_iota(jnp.int32, (t, t), 0)
    col = lax.broadcasted_iota(jnp.int32, (t, t), 1)
    outs = []
    for hh in range(2):
        q = qf_ref[0, :, hh * LANES:(hh + 1) * LANES]
        m_scr[...] = jnp.full_like(m_scr, -jnp.inf)
        l_scr[...] = jnp.zeros_like(l_scr)
        acc_scr[...] = jnp.zeros_like(acc_scr)

        def block(j, diagonal, q=q, hh=hh):
            start = pl.multiple_of(j * t, t)
            k = kf_ref[0, pl.ds(start, t), hh * LANES:(hh + 1) * LANES]
            v = v_ref[0, pl.ds(start, t), :]
            s = lax.dot_general(q, k, NT_DIMS, preferred_element_type=F32)
            if diagonal:
                s = jnp.where(row >= col, s, -jnp.inf)
            m_old = m_scr[...]
            m_new = jnp.maximum(m_old, jnp.max(s, axis=-1, keepdims=True))
            alpha = jnp.exp2(m_old - m_new)
            p = jnp.exp2(s - m_new)
            l_scr[...] = alpha * l_scr[...] + jnp.sum(p, axis=-1, keepdims=True)
            acc_scr[...] = alpha * acc_scr[...] + _dot(p.astype(BF16), v)
            m_scr[...] = m_new

        def body(j, carry):
            block(j, False)
            return carry

        lax.fori_loop(0, i, body, 0)
        block(i, True)
        outs.append(acc_scr[...] / l_scr[...])
    lane = lax.broadcasted_iota(jnp.int32, outs[0].shape, 1)
    o_ref[0] = jnp.where(lane < HEAD_DIM, outs[0], outs[1]).astype(o_ref.dtype)


def _fox(qf, kf, fv):
    b, s, width = fv.shape
    pairs = width // LANES
    t = ATTN_TILE
    return pl.pallas_call(
        _fox_kernel,
        grid=(b, pairs, s // t),
        in_specs=[
            pl.BlockSpec((1, t, 2 * LANES), lambda bi, hp, i: (bi, i, hp)),
            pl.BlockSpec((1, s, 2 * LANES), lambda bi, hp, i: (bi, 0, hp)),
            pl.BlockSpec((1, s, LANES), lambda bi, hp, i: (bi, 0, hp)),
        ],
        out_specs=pl.BlockSpec((1, t, LANES), lambda bi, hp, i: (bi, i, hp)),
        out_shape=jax.ShapeDtypeStruct((b, s, width), BF16),
        scratch_shapes=[pltpu.VMEM((t, 1), F32), pltpu.VMEM((t, 1), F32),
                        pltpu.VMEM((t, LANES), F32)],
        compiler_params=_params("parallel", "parallel", "arbitrary"),
        name="fox_attention",
    )(qf, kf, fv)


def _sb_kernel(q_ref, k_ref, v_ref, m_ref, o_ref, r_scr, acc_scr):
    i = pl.program_id(2)
    t = q_ref.shape[1]
    row = lax.broadcasted_iota(jnp.int32, (t, t), 0)
    col = lax.broadcasted_iota(jnp.int32, (t, t), 1)
    strict = col < row
    qlane = lax.broadcasted_iota(jnp.int32, (t, LANES), 1)
    q2 = q_ref[0]
    outs = []
    for hh in range(2):
        q = jnp.where((qlane < HEAD_DIM) == (hh == 0), q2, jnp.zeros_like(q2))
        r_scr[...] = jnp.zeros_like(r_scr)
        acc_scr[...] = jnp.zeros_like(acc_scr)

        def block(j, diagonal, q=q):
            start = pl.multiple_of(j * t, t)
            k = k_ref[0, pl.ds(start, t), :]
            v = v_ref[0, pl.ds(start, t), :]
            z = lax.dot_general(q, k, NT_DIMS, preferred_element_type=F32)
            big_l = jnp.log2(1.0 + jnp.exp2(-jnp.abs(z)))
            sp = jnp.maximum(z, 0.0) + big_l
            lb = jnp.minimum(z, 0.0) - big_l
            if diagonal:
                sp = jnp.where(strict, sp, 0.0)
            after = _dot(sp.astype(BF16), m_ref[...]) + r_scr[...]
            a = jnp.exp2(lb - after)
            if diagonal:
                a = jnp.where(strict, a, 0.0)
            acc_scr[...] += _dot(a.astype(BF16), v)
            r_scr[...] += jnp.sum(sp, axis=-1, keepdims=True)

        block(i, True)

        def body(n, carry):
            block(i - 1 - n, False)
            return carry

        lax.fori_loop(0, i, body, 0)
        outs.append(acc_scr[...])
    o_ref[0] = jnp.where(qlane < HEAD_DIM, outs[0], outs[1]).astype(o_ref.dtype)


def _sb(sq, sk, sv, later_mat):
    b, s, width = sq.shape
    pairs = width // LANES
    t = ATTN_TILE
    return pl.pallas_call(
        _sb_kernel,
        grid=(b, pairs, s // t),
        in_specs=[
            pl.BlockSpec((1, t, LANES), lambda bi, hp, i: (bi, i, hp)),
            pl.BlockSpec((1, s, LANES), lambda bi, hp, i: (bi, 0, hp)),
            pl.BlockSpec((1, s, LANES), lambda bi, hp, i: (bi, 0, hp)),
            _const_spec(later_mat.shape),
        ],
        out_specs=pl.BlockSpec((1, t, LANES), lambda bi, hp, i: (bi, i, hp)),
        out_shape=jax.ShapeDtypeStruct((b, s, width), BF16),
        scratch_shapes=[pltpu.VMEM((t, 1), F32), pltpu.VMEM((t, LANES), F32)],
        compiler_params=_params("parallel", "parallel", "arbitrary"),
        name="sb_attention",
    )(sq, sk, sv, later_mat)


def _mixout_kernel(x_ref, yf_ref, ys_ref, gf_ref, gs_ref, wbf_ref, wbs_ref, wo_ref, o_ref):
    merged = (gf_ref[...].astype(F32) * _dot(yf_ref[...], wbf_ref[...])
              + gs_ref[...].astype(F32) * _dot(ys_ref[...], wbs_ref[...]))
    o_ref[...] = x_ref[...] + _dot(merged.astype(BF16), wo_ref[...])


def _mixout(x2d, yf, ys, gf, gs, wbf, wbs, wo):
    n, d = x2d.shape
    w = yf.shape[1]
    tm = TOKEN_TILE

    def tok(c):
        return pl.BlockSpec((tm, c), lambda i: (i, 0))

    return pl.pallas_call(
        _mixout_kernel,
        grid=(n // tm,),
        in_specs=[tok(d), tok(w), tok(w), tok(d), tok(d),
                  _const_spec(wbf.shape), _const_spec(wbs.shape), _const_spec(wo.shape)],
        out_specs=tok(d),
        out_shape=jax.ShapeDtypeStruct((n, d), F32),
        compiler_params=_params("parallel"),
        name="mixout",
    )(x2d, yf, ys, gf, gs, wbf, wbs, wo)


def _ple_kernel(x_ref, p_ref, g_ref, wg_ref, wp_ref, o_ref):
    x = x_ref[...]
    h = _rms_rows(x, g_ref[...]).astype(BF16)
    gate = _sigmoid(_dot(h, wg_ref[...]))
    o_ref[...] = x + gate * _dot(p_ref[...].astype(BF16), wp_ref[...])


def _ple(x2d, p2d, g, wg, wp):
    n, d = x2d.shape
    pd = p2d.shape[1]
    tm = TOKEN_TILE
    return pl.pallas_call(
        _ple_kernel,
        grid=(n // tm,),
        in_specs=[pl.BlockSpec((tm, d), lambda i: (i, 0)), pl.BlockSpec((tm, pd), lambda i: (i, 0)),
                  _const_spec(g.shape), _const_spec(wg.shape), _const_spec(wp.shape)],
        out_specs=pl.BlockSpec((tm, d), lambda i: (i, 0)),
        out_shape=jax.ShapeDtypeStruct((n, d), F32),
        compiler_params=_params("parallel"),
        name="ple",
    )(x2d, p2d, g, wg, wp)


@functools.lru_cache(maxsize=None)
def _layout_constants(n_heads, tile):
    spread = n_heads * LANES
    sel = np.zeros((spread, LANES), np.float32)
    pq = np.zeros((3 * LANES, spread), np.float32)
    pk = np.zeros((3 * LANES, spread), np.float32)
    cq = np.zeros((1, spread), np.float32)
    ck = np.zeros((1, spread), np.float32)
    for h in range(n_heads):
        base = h * LANES
        sel[base:base + HEAD_DIM, h] = 1.0 / HEAD_DIM
        for part in range(3):
            pq[part * LANES + h, base + HEAD_DIM + part] = 1.0
            pk[part * LANES + h, base + HEAD_DIM + 3 + part] = -1.0
            cq[0, base + HEAD_DIM + 3 + part] = 1.0
            ck[0, base + HEAD_DIM + part] = 1.0
    selt = (sel.T > 0).astype(np.float32)
    tri = np.tril(np.ones((tile, tile), np.float32))
    return sel, selt, tri, pq, pk, cq, ck


@functools.lru_cache(maxsize=None)
def _later_matrix(tile):
    return np.tril(np.ones((tile, tile), np.float32), k=-1)


def _spread_cols(w, n_heads):
    d = w.shape[0]
    w = w.reshape(d, n_heads, HEAD_DIM)
    w = jnp.pad(w, ((0, 0), (0, 0), (0, LANES - HEAD_DIM)))
    return w.reshape(d, n_heads * LANES)


def kernel(x, p, ffn1_norm, ffn1_w_gate, ffn1_w_up, ffn1_w_down, mix_norm, w_in, forget_bias, q_norm, k_norm, w_branch_fox, w_branch_sb, w_out, ffn2_norm, ffn2_w_gate, ffn2_w_up, ffn2_w_down, ple_norm, w_ple_gate, w_ple_proj):
    b, s, d = x.shape
    depth = w_in.shape[0]
    fox_w = w_branch_fox.shape[1]
    sb_w = w_branch_sb.shape[1]
    n_heads = forget_bias.shape[1]
    assert fox_w == n_heads * HEAD_DIM and sb_w == fox_w and n_heads <= LANES
    assert s % TOKEN_TILE == 0 and s % ATTN_TILE == 0 and fox_w % LANES == 0

    sel, selt, tri, pq, pk, cq, ck = _layout_constants(n_heads, TOKEN_TILE)
    later = jnp.asarray(_later_matrix(ATTN_TILE), BF16)

    xf = x.reshape(b * s, d)
    for i in range(depth):
        xf = _ffn(xf, ffn1_norm[i][None], ffn1_w_gate[i].astype(BF16),
                  ffn1_w_up[i].astype(BF16), ffn1_w_down[i].astype(BF16))

        w = w_in[i]
        o = 0
        wq = _spread_cols(w[:, o:o + fox_w], n_heads); o += fox_w
        wk = _spread_cols(w[:, o:o + fox_w], n_heads); o += fox_w
        wv = w[:, o:o + fox_w]; o += fox_w
        wf = jnp.pad(w[:, o:o + n_heads], ((0, 0), (0, LANES - n_heads))); o += n_heads
        wsb = w[:, o:o + 3 * sb_w]; o += 3 * sb_w
        wgate = w[:, o:o + 2 * d]
        fbias = jnp.pad(forget_bias[i][None], ((0, 0), (0, LANES - n_heads)))
        gq = _spread_cols(q_norm[i][None].repeat(n_heads, 0).reshape(1, fox_w), n_heads)
        gk = _spread_cols(k_norm[i][None].repeat(n_heads, 0).reshape(1, fox_w), n_heads)
        gq = gq * (LOG2E * HEAD_DIM ** -0.5)
        consts = [mix_norm[i][None], wq.astype(BF16), wk.astype(BF16), wv.astype(BF16),
                  wsb.astype(BF16), wgate.astype(BF16), wf.astype(BF16), fbias, gq, gk,
                  jnp.asarray(sel, BF16), jnp.asarray(selt, BF16), jnp.asarray(tri, BF16),
                  jnp.asarray(pq, BF16), jnp.asarray(pk, BF16), jnp.asarray(cq), jnp.asarray(ck)]
        qf, kf, fv, sq, sk, sv, gf, gs = _inproj(xf.reshape(b, s, d), consts)

        y_fox = _fox(qf, kf, fv)
        y_sb = _sb(sq, sk, sv, later)

        xf = _mixout(xf, y_fox.reshape(b * s, fox_w), y_sb.reshape(b * s, sb_w),
                     gf.reshape(b * s, d), gs.reshape(b * s, d),
                     w_branch_fox[i].astype(BF16), w_branch_sb[i].astype(BF16),
                     w_out[i].astype(BF16))

        xf = _ffn(xf, ffn2_norm[i][None], ffn2_w_gate[i].astype(BF16),
                  ffn2_w_up[i].astype(BF16), ffn2_w_down[i].astype(BF16))

        xf = _ple(xf, p[i].reshape(b * s, -1), ple_norm[i][None],
                  w_ple_gate[i].astype(BF16), w_ple_proj[i].astype(BF16))
    return xf.reshape(b, s, d)
```

```python
import functools

import jax
import jax.numpy as jnp
import numpy as np
from jax import lax
from jax.experimental import pallas as pl
from jax.experimental.pallas import tpu as pltpu

F32 = jnp.float32
BF16 = jnp.bfloat16

EPS = 1e-6
HEAD_DIM = 64
LOG2E = 1.4426950408889634
LANES = 128
VMEM_LIMIT_BYTES = 56 * 1024 * 1024

TOKEN_TILE = 512
ATTN_TILE = 256

NT_DIMS = (((1,), (1,)), ((), ()))


def _dot(a, b):
    return jnp.dot(a, b, preferred_element_type=F32)


def _dot_nt(a, b):
    return lax.dot_general(a, b, NT_DIMS, preferred_element_type=F32)


def _split2(a):
    hi = a.astype(BF16)
    lo = (a - hi.astype(F32)).astype(BF16)
    return hi, lo


def _split3(a):
    p1 = a.astype(BF16)
    r1 = a - p1.astype(F32)
    p2 = r1.astype(BF16)
    r2 = r1 - p2.astype(F32)
    return p1, p2, r2.astype(BF16)


def _rms_rows(x, g):
    ms = jnp.mean(x * x, axis=-1, keepdims=True)
    return x * lax.rsqrt(ms + EPS) * g


def _sigmoid(x):
    return 1.0 / (1.0 + jnp.exp(-x))


def _params(*sem):
    return pltpu.CompilerParams(dimension_semantics=sem, vmem_limit_bytes=VMEM_LIMIT_BYTES)


def _const_spec(shape):
    nd = len(shape)
    return pl.BlockSpec(shape, lambda *_: (0,) * nd)


def _ffn_kernel(x_ref, g_ref, wg_ref, wu_ref, wd_ref, o_ref, h_scr, acc_scr):
    j = pl.program_id(1)

    @pl.when(j == 0)
    def _():
        h_scr[...] = _rms_rows(x_ref[...], g_ref[...]).astype(BF16)
        acc_scr[...] = jnp.zeros_like(acc_scr)

    h = h_scr[...]
    a = _dot(h, wg_ref[...])
    u = _dot(h, wu_ref[...])
    act = (a * _sigmoid(a) * u).astype(BF16)
    acc_scr[...] += _dot(act, wd_ref[...])

    @pl.when(j == pl.num_programs(1) - 1)
    def _():
        o_ref[...] = x_ref[...] + 0.5 * acc_scr[...]


def _ffn(x2d, g, wg, wu, wd):
    n, d = x2d.shape
    dff = wg.shape[1]
    tf = dff // 2
    tm = TOKEN_TILE
    return pl.pallas_call(
        _ffn_kernel,
        grid=(n // tm, dff // tf),
        in_specs=[
            pl.BlockSpec((tm, d), lambda i, j: (i, 0)),
            pl.BlockSpec((1, d), lambda i, j: (0, 0)),
            pl.BlockSpec((d, tf), lambda i, j: (0, j)),
            pl.BlockSpec((d, tf), lambda i, j: (0, j)),
            pl.BlockSpec((tf, d), lambda i, j: (j, 0)),
        ],
        out_specs=pl.BlockSpec((tm, d), lambda i, j: (i, 0)),
        out_shape=jax.ShapeDtypeStruct((n, d), F32),
        scratch_shapes=[pltpu.VMEM((tm, d), BF16), pltpu.VMEM((tm, d), F32)],
        compiler_params=_params("parallel", "arbitrary"),
        name="ffn",
    )(x2d, g, wg, wu, wd)


def _inproj_kernel(x_ref, g_ref, wq_ref, wk_ref, wvt_ref, wsqk_ref, wsvt_ref, wgate_ref, wf_ref,
                   fbias_ref, gq_ref, gk_ref, sel_ref, selt_ref, tri_ref,
                   pq_ref, pk_ref, cq_ref, ck_ref,
                   qf_ref, kf_ref, fvt_ref, sq_ref, sk_ref, svt_ref, gf_ref, gs_ref,
                   carry_scr):
    @pl.when(pl.program_id(1) == 0)
    def _():
        carry_scr[...] = jnp.zeros_like(carry_scr)

    h = _rms_rows(x_ref[0], g_ref[...]).astype(BF16)

    yf = _dot(h, wf_ref[...]) + fbias_ref[...]
    lf = jnp.minimum(yf, 0.0) - jnp.log1p(jnp.exp(-jnp.abs(yf)))
    lane = lax.broadcasted_iota(jnp.int32, lf.shape, 1)
    n_heads = qf_ref.shape[2] // LANES
    lf = jnp.where(lane < n_heads, lf, 0.0)
    c = _dot(tri_ref[...], jnp.concatenate(_split3(lf), axis=1))
    f_cum = c[:, :LANES] + c[:, LANES:2 * LANES] + c[:, 2 * LANES:] + carry_scr[...]
    carry_scr[...] = f_cum[-1:, :]
    fparts = jnp.concatenate(_split3(f_cum * LOG2E), axis=1)

    def headnorm(y, gain):
        hi, lo = _split2(y * y)
        ms = _dot(hi, sel_ref[...]) + _dot(lo, sel_ref[...])
        rhi, rlo = _split2(lax.rsqrt(ms + EPS))
        return y * (_dot(rhi, selt_ref[...]) + _dot(rlo, selt_ref[...])) * gain

    q = headnorm(_dot(h, wq_ref[...]), gq_ref[...])
    qf_ref[0] = (q + _dot(fparts, pq_ref[...]) + cq_ref[...]).astype(BF16)
    k = headnorm(_dot(h, wk_ref[...]), gk_ref[...])
    kf_ref[0] = (k + _dot(fparts, pk_ref[...]) + ck_ref[...]).astype(BF16)

    t = fvt_ref.shape[3]
    fvt = _dot_nt(wvt_ref[...], h).astype(BF16)
    svt = _dot_nt(wsvt_ref[...], h).astype(BF16)
    for c in range(fvt_ref.shape[1]):
        fvt_ref[0, c] = fvt[:, c * t:(c + 1) * t]
        svt_ref[0, c] = svt[:, c * t:(c + 1) * t]

    width = sq_ref.shape[2]
    ysb = _dot(h, wsqk_ref[...])
    sq_ref[0] = (ysb[:, :width] * (LOG2E * HEAD_DIM ** -0.5)).astype(BF16)
    sk_ref[0] = ysb[:, width:].astype(BF16)

    d = gf_ref.shape[2]
    yg = _sigmoid(_dot(h, wgate_ref[...]))
    gf_ref[0] = yg[:, :d].astype(BF16)
    gs_ref[0] = yg[:, d:].astype(BF16)


def _inproj(x3d, consts):
    b, s, d = x3d.shape
    tm = TOKEN_TILE
    t = ATTN_TILE
    spread = consts[1].shape[1]
    width = consts[3].shape[0]
    ins = [x3d] + list(consts)
    in_specs = [pl.BlockSpec((1, tm, d), lambda bi, i: (bi, i, 0))]
    in_specs += [_const_spec(c.shape) for c in consts]

    def tok(n):
        return pl.BlockSpec((1, tm, n), lambda bi, i: (bi, i, 0))

    def shp(n):
        return jax.ShapeDtypeStruct((b, s, n), BF16)

    vt_spec = pl.BlockSpec((1, tm // t, width, t), lambda bi, i: (bi, i, 0, 0))
    vt_shape = jax.ShapeDtypeStruct((b, s // t, width, t), BF16)

    return pl.pallas_call(
        _inproj_kernel,
        grid=(b, s // tm),
        in_specs=in_specs,
        out_specs=[tok(spread), tok(spread), vt_spec, tok(width), tok(width), vt_spec,
                   tok(d), tok(d)],
        out_shape=[shp(spread), shp(spread), vt_shape, shp(width), shp(width), vt_shape,
                   shp(d), shp(d)],
        scratch_shapes=[pltpu.VMEM((1, LANES), F32)],
        compiler_params=_params("parallel", "arbitrary"),
        name="inproj",
    )(*ins)


HEADS_PER_STEP = 8
PAIRS_PER_STEP = HEADS_PER_STEP // 2


def _key_query_iota(t):
    return (lax.broadcasted_iota(jnp.int32, (t, t), 0), lax.broadcasted_iota(jnp.int32, (t, t), 1))


def _store_heads(o_ref, outs):
    sub = lax.broadcasted_iota(jnp.int32, outs[0].shape, 0)
    for pr in range(PAIRS_PER_STEP):
        pair = jnp.where(sub < HEAD_DIM, outs[2 * pr], outs[2 * pr + 1])
        o_ref[0, :, pr * LANES:(pr + 1) * LANES] = pair.T.astype(o_ref.dtype)


def _fox_kernel(qf_ref, kf_ref, vt_ref, o_ref, m_scr, l_scr, acc_scr):
    i = pl.program_id(2)
    t = qf_ref.shape[1]
    heads = range(HEADS_PER_STEP)
    key, qry = _key_query_iota(t)
    causal = key <= qry
    q = [qf_ref[0, :, h * LANES:(h + 1) * LANES] for h in heads]
    m_scr[...] = jnp.full_like(m_scr, -jnp.inf)
    l_scr[...] = jnp.zeros_like(l_scr)
    acc_scr[...] = jnp.zeros_like(acc_scr)

    def block(j, diagonal):
        start = pl.multiple_of(j * t, t)
        s = [_dot_nt(kf_ref[0, pl.ds(start, t), h * LANES:(h + 1) * LANES], q[h])
             for h in heads]
        if diagonal:
            s = [jnp.where(causal, sh, -jnp.inf) for sh in s]
        m_old = [m_scr[h] for h in heads]
        m_new = [jnp.maximum(m_old[h], jnp.max(s[h], axis=0, keepdims=True)) for h in heads]
        alpha = [jnp.exp2(m_old[h] - m_new[h]) for h in heads]
        p = [jnp.exp2(s[h] - m_new[h]) for h in heads]
        for h in heads:
            l_scr[h] = alpha[h] * l_scr[h] + jnp.sum(p[h], axis=0, keepdims=True)
            m_scr[h] = m_new[h]
        pv = [_dot(vt_ref[0, j, (h // 2) * LANES:(h // 2 + 1) * LANES, :], p[h].astype(BF16))
              for h in heads]
        for h in heads:
            acc_scr[h] = alpha[h] * acc_scr[h] + pv[h]

    def body(j, carry):
        block(j, False)
        return carry

    lax.fori_loop(0, i, body, 0)
    block(i, True)
    _store_heads(o_ref, [acc_scr[h] * (1.0 / l_scr[h]) for h in heads])


def _attn_scratch(t):
    n = HEADS_PER_STEP
    return [pltpu.VMEM((n, 1, t), F32), pltpu.VMEM((n, 1, t), F32), pltpu.VMEM((n, LANES, t), F32)]


def _fox(qf, kf, fvt):
    b, nblk, width, t = fvt.shape
    s = nblk * t
    vw = PAIRS_PER_STEP * LANES
    qw = HEADS_PER_STEP * LANES
    return pl.pallas_call(
        _fox_kernel,
        grid=(b, width // vw, nblk),
        in_specs=[
            pl.BlockSpec((1, t, qw), lambda bi, hg, i: (bi, i, hg)),
            pl.BlockSpec((1, s, qw), lambda bi, hg, i: (bi, 0, hg)),
            pl.BlockSpec((1, nblk, vw, t), lambda bi, hg, i: (bi, 0, hg, 0)),
        ],
        out_specs=pl.BlockSpec((1, t, vw), lambda bi, hg, i: (bi, i, hg)),
        out_shape=jax.ShapeDtypeStruct((b, s, width), BF16),
        scratch_shapes=_attn_scratch(t),
        compiler_params=_params("parallel", "parallel", "arbitrary"),
        name="fox_attention",
    )(qf, kf, fvt)


def _sb_kernel(q_ref, k_ref, vt_ref, u_ref, o_ref, r_scr, acc_scr):
    i = pl.program_id(2)
    t = q_ref.shape[1]
    heads = range(HEADS_PER_STEP)
    key, qry = _key_query_iota(t)
    strict = key < qry
    qlane = lax.broadcasted_iota(jnp.int32, (t, LANES), 1)
    q = []
    for h in heads:
        qpair = q_ref[0, :, (h // 2) * LANES:(h // 2 + 1) * LANES]
        q.append(jnp.where((qlane < HEAD_DIM) == (h % 2 == 0), qpair, jnp.zeros_like(qpair)))
    r_scr[...] = jnp.zeros_like(r_scr)
    acc_scr[...] = jnp.zeros_like(acc_scr)

    def block(j, diagonal):
        start = pl.multiple_of(j * t, t)
        z = [_dot_nt(k_ref[0, pl.ds(start, t), (h // 2) * LANES:(h // 2 + 1) * LANES], q[h])
             for h in heads]
        sp = [jnp.maximum(zh, 0.0) + jnp.log2(1.0 + jnp.exp2(-jnp.abs(zh))) for zh in z]
        log_beta = [z[h] - sp[h] for h in heads]
        if diagonal:
            sp = [jnp.where(strict, sh, 0.0) for sh in sp]
        after = [_dot(u_ref[...], sh.astype(BF16)) for sh in sp]
        a = [jnp.exp2(log_beta[h] - after[h]) for h in heads]
        if diagonal:
            a = [jnp.where(strict, ah, 0.0) for ah in a]
        pv = [_dot(vt_ref[0, j, (h // 2) * LANES:(h // 2 + 1) * LANES, :], a[h].astype(BF16))
              for h in heads]
        for h in heads:
            r = r_scr[h]
            acc_scr[h] += jnp.exp2(-r) * pv[h]
            r_scr[h] = r + jnp.sum(sp[h], axis=0, keepdims=True)

    block(i, True)

    def body(n, carry):
        block(i - 1 - n, False)
        return carry

    lax.fori_loop(0, i, body, 0)
    _store_heads(o_ref, [acc_scr[h] for h in heads])


def _sb(sq, sk, svt, later_mat):
    b, nblk, width, t = svt.shape
    s = nblk * t
    vw = PAIRS_PER_STEP * LANES
    return pl.pallas_call(
        _sb_kernel,
        grid=(b, width // vw, nblk),
        in_specs=[
            pl.BlockSpec((1, t, vw), lambda bi, hg, i: (bi, i, hg)),
            pl.BlockSpec((1, s, vw), lambda bi, hg, i: (bi, 0, hg)),
            pl.BlockSpec((1, nblk, vw, t), lambda bi, hg, i: (bi, 0, hg, 0)),
            _const_spec(later_mat.shape),
        ],
        out_specs=pl.BlockSpec((1, t, vw), lambda bi, hg, i: (bi, i, hg)),
        out_shape=jax.ShapeDtypeStruct((b, s, width), BF16),
        scratch_shapes=_attn_scratch(t)[1:],
        compiler_params=_params("parallel", "parallel", "arbitrary"),
        name="sb_attention",
    )(sq, sk, svt, later_mat)


def _mixout_kernel(x_ref, yf_ref, ys_ref, gf_ref, gs_ref, wbf_ref, wbs_ref, wo_ref, o_ref):
    merged = (gf_ref[...].astype(F32) * _dot(yf_ref[...], wbf_ref[...])
              + gs_ref[...].astype(F32) * _dot(ys_ref[...], wbs_ref[...]))
    o_ref[...] = x_ref[...] + _dot(merged.astype(BF16), wo_ref[...])


def _mixout(x2d, yf, ys, gf, gs, wbf, wbs, wo):
    n, d = x2d.shape
    w = yf.shape[1]
    tm = TOKEN_TILE

    def tok(c):
        return pl.BlockSpec((tm, c), lambda i: (i, 0))

    return pl.pallas_call(
        _mixout_kernel,
        grid=(n // tm,),
        in_specs=[tok(d), tok(w), tok(w), tok(d), tok(d),
                  _const_spec(wbf.shape), _const_spec(wbs.shape), _const_spec(wo.shape)],
        out_specs=tok(d),
        out_shape=jax.ShapeDtypeStruct((n, d), F32),
        compiler_params=_params("parallel"),
        name="mixout",
    )(x2d, yf, ys, gf, gs, wbf, wbs, wo)


def _ple_kernel(x_ref, p_ref, g_ref, wg_ref, wp_ref, o_ref):
    x = x_ref[...]
    h = _rms_rows(x, g_ref[...]).astype(BF16)
    gate = _sigmoid(_dot(h, wg_ref[...]))
    o_ref[...] = x + gate * _dot(p_ref[...].astype(BF16), wp_ref[...])


def _ple(x2d, p2d, g, wg, wp):
    n, d = x2d.shape
    pd = p2d.shape[1]
    tm = TOKEN_TILE
    return pl.pallas_call(
        _ple_kernel,
        grid=(n // tm,),
        in_specs=[pl.BlockSpec((tm, d), lambda i: (i, 0)), pl.BlockSpec((tm, pd), lambda i: (i, 0)),
                  _const_spec(g.shape), _const_spec(wg.shape), _const_spec(wp.shape)],
        out_specs=pl.BlockSpec((tm, d), lambda i: (i, 0)),
        out_shape=jax.ShapeDtypeStruct((n, d), F32),
        compiler_params=_params("parallel"),
        name="ple",
    )(x2d, p2d, g, wg, wp)


@functools.lru_cache(maxsize=None)
def _layout_constants(n_heads, tile):
    spread = n_heads * LANES
    sel = np.zeros((spread, LANES), np.float32)
    pq = np.zeros((3 * LANES, spread), np.float32)
    pk = np.zeros((3 * LANES, spread), np.float32)
    cq = np.zeros((1, spread), np.float32)
    ck = np.zeros((1, spread), np.float32)
    for h in range(n_heads):
        base = h * LANES
        sel[base:base + HEAD_DIM, h] = 1.0 / HEAD_DIM
        for part in range(3):
            pq[part * LANES + h, base + HEAD_DIM + part] = 1.0
            pk[part * LANES + h, base + HEAD_DIM + 3 + part] = -1.0
            cq[0, base + HEAD_DIM + 3 + part] = 1.0
            ck[0, base + HEAD_DIM + part] = 1.0
    selt = (sel.T > 0).astype(np.float32)
    tri = np.tril(np.ones((tile, tile), np.float32))
    return sel, selt, tri, pq, pk, cq, ck


@functools.lru_cache(maxsize=None)
def _later_matrix(tile):
    return np.triu(np.ones((tile, tile), np.float32), k=1)


def _spread_cols(w, n_heads):
    d = w.shape[0]
    w = w.reshape(d, n_heads, HEAD_DIM)
    w = jnp.pad(w, ((0, 0), (0, 0), (0, LANES - HEAD_DIM)))
    return w.reshape(d, n_heads * LANES)


def kernel(x, p, ffn1_norm, ffn1_w_gate, ffn1_w_up, ffn1_w_down, mix_norm, w_in, forget_bias, q_norm, k_norm, w_branch_fox, w_branch_sb, w_out, ffn2_norm, ffn2_w_gate, ffn2_w_up, ffn2_w_down, ple_norm, w_ple_gate, w_ple_proj):
    b, s, d = x.shape
    depth = w_in.shape[0]
    fox_w = w_branch_fox.shape[1]
    sb_w = w_branch_sb.shape[1]
    n_heads = forget_bias.shape[1]
    assert fox_w == n_heads * HEAD_DIM and sb_w == fox_w and n_heads <= LANES
    assert s % TOKEN_TILE == 0 and TOKEN_TILE % ATTN_TILE == 0 and fox_w % LANES == 0

    sel, selt, tri, pq, pk, cq, ck = _layout_constants(n_heads, TOKEN_TILE)
    later = jnp.asarray(_later_matrix(ATTN_TILE), BF16)

    xf = x.reshape(b * s, d)
    for i in range(depth):
        xf = _ffn(xf, ffn1_norm[i][None], ffn1_w_gate[i].astype(BF16),
                  ffn1_w_up[i].astype(BF16), ffn1_w_down[i].astype(BF16))

        w = w_in[i]
        o = 0
        wq = _spread_cols(w[:, o:o + fox_w], n_heads); o += fox_w
        wk = _spread_cols(w[:, o:o + fox_w], n_heads); o += fox_w
        wvt = w[:, o:o + fox_w].T; o += fox_w
        wf = jnp.pad(w[:, o:o + n_heads], ((0, 0), (0, LANES - n_heads))); o += n_heads
        wsqk = w[:, o:o + 2 * sb_w]; o += 2 * sb_w
        wsvt = w[:, o:o + sb_w].T; o += sb_w
        wgate = w[:, o:o + 2 * d]
        fbias = jnp.pad(forget_bias[i][None], ((0, 0), (0, LANES - n_heads)))
        gq = _spread_cols(q_norm[i][None].repeat(n_heads, 0).reshape(1, fox_w), n_heads)
        gk = _spread_cols(k_norm[i][None].repeat(n_heads, 0).reshape(1, fox_w), n_heads)
        gq = gq * (LOG2E * HEAD_DIM ** -0.5)
        consts = [mix_norm[i][None], wq.astype(BF16), wk.astype(BF16), wvt.astype(BF16),
                  wsqk.astype(BF16), wsvt.astype(BF16), wgate.astype(BF16), wf.astype(BF16),
                  fbias, gq, gk,
                  jnp.asarray(sel, BF16), jnp.asarray(selt, BF16), jnp.asarray(tri, BF16),
                  jnp.asarray(pq, BF16), jnp.asarray(pk, BF16), jnp.asarray(cq), jnp.asarray(ck)]
        qf, kf, fvt, sq, sk, svt, gf, gs = _inproj(xf.reshape(b, s, d), consts)

        y_fox = _fox(qf, kf, fvt)
        y_sb = _sb(sq, sk, svt, later)

        xf = _mixout(xf, y_fox.reshape(b * s, fox_w), y_sb.reshape(b * s, sb_w),
                     gf.reshape(b * s, d), gs.reshape(b * s, d),
                     w_branch_fox[i].astype(BF16), w_branch_sb[i].astype(BF16),
                     w_out[i].astype(BF16))

        xf = _ffn(xf, ffn2_norm[i][None], ffn2_w_gate[i].astype(BF16),
                  ffn2_w_up[i].astype(BF16), ffn2_w_down[i].astype(BF16))

        xf = _ple(xf, p[i].reshape(b * s, -1), ple_norm[i][None],
                  w_ple_gate[i].astype(BF16), w_ple_proj[i].astype(BF16))
    return xf.reshape(b, s, d)
```

```python
import functools

import jax
import jax.numpy as jnp
import numpy as np
from jax import lax
from jax.experimental import pallas as pl
from jax.experimental.pallas import tpu as pltpu

F32 = jnp.float32
BF16 = jnp.bfloat16

EPS = 1e-6
HEAD_DIM = 64
LOG2E = 1.4426950408889634
LANES = 128
SUBLANES = 8
ZERO_WEIGHT_BITS = 152.0
VMEM_LIMIT_BYTES = 56 * 1024 * 1024

TOKEN_TILE = 512
ATTN_TILE = 256

NT_DIMS = (((1,), (1,)), ((), ()))


def _dot(a, b):
    return jnp.dot(a, b, preferred_element_type=F32)


def _dot_nt(a, b):
    return lax.dot_general(a, b, NT_DIMS, preferred_element_type=F32)


def _split2(a):
    hi = a.astype(BF16)
    lo = (a - hi.astype(F32)).astype(BF16)
    return hi, lo


def _split3(a):
    p1 = a.astype(BF16)
    r1 = a - p1.astype(F32)
    p2 = r1.astype(BF16)
    r2 = r1 - p2.astype(F32)
    return p1, p2, r2.astype(BF16)


def _rms_rows(x, g):
    ms = jnp.mean(x * x, axis=-1, keepdims=True)
    return x * lax.rsqrt(ms + EPS) * g


def _sigmoid(x):
    return 1.0 / (1.0 + jnp.exp(-x))


def _params(*sem):
    return pltpu.CompilerParams(dimension_semantics=sem, vmem_limit_bytes=VMEM_LIMIT_BYTES)


def _const_spec(shape):
    nd = len(shape)
    return pl.BlockSpec(shape, lambda *_: (0,) * nd)


def _ffn_kernel(x_ref, g_ref, wg_ref, wu_ref, wd_ref, o_ref, h_scr, acc_scr):
    j = pl.program_id(1)

    @pl.when(j == 0)
    def _():
        h_scr[...] = _rms_rows(x_ref[...], g_ref[...]).astype(BF16)
        acc_scr[...] = jnp.zeros_like(acc_scr)

    h = h_scr[...]
    a = _dot(h, wg_ref[...])
    u = _dot(h, wu_ref[...])
    act = (a * _sigmoid(a) * u).astype(BF16)
    acc_scr[...] += _dot(act, wd_ref[...])

    @pl.when(j == pl.num_programs(1) - 1)
    def _():
        o_ref[...] = x_ref[...] + 0.5 * acc_scr[...]


def _ffn(x2d, g, wg, wu, wd):
    n, d = x2d.shape
    dff = wg.shape[1]
    tf = dff // 2
    tm = TOKEN_TILE
    return pl.pallas_call(
        _ffn_kernel,
        grid=(n // tm, dff // tf),
        in_specs=[
            pl.BlockSpec((tm, d), lambda i, j: (i, 0)),
            pl.BlockSpec((1, d), lambda i, j: (0, 0)),
            pl.BlockSpec((d, tf), lambda i, j: (0, j)),
            pl.BlockSpec((d, tf), lambda i, j: (0, j)),
            pl.BlockSpec((tf, d), lambda i, j: (j, 0)),
        ],
        out_specs=pl.BlockSpec((tm, d), lambda i, j: (i, 0)),
        out_shape=jax.ShapeDtypeStruct((n, d), F32),
        scratch_shapes=[pltpu.VMEM((tm, d), BF16), pltpu.VMEM((tm, d), F32)],
        compiler_params=_params("parallel", "arbitrary"),
        name="ffn",
    )(x2d, g, wg, wu, wd)


def _inproj_kernel(x_ref, g_ref, wq_ref, wk_ref, wvt_ref, wsqk_ref, wsvt_ref, wgate_ref, wf_ref,
                   fbias_ref, gq_ref, gk_ref, sel_ref, selt_ref, tri_ref,
                   pq_ref, pk_ref, cq_ref, ck_ref,
                   qf_ref, kf_ref, fvt_ref, sq_ref, sk_ref, svt_ref, gf_ref, gs_ref, fend_ref,
                   carry_scr):
    @pl.when(pl.program_id(1) == 0)
    def _():
        carry_scr[...] = jnp.zeros_like(carry_scr)

    h = _rms_rows(x_ref[0], g_ref[...]).astype(BF16)

    yf = _dot(h, wf_ref[...]) + fbias_ref[...]
    lf = jnp.minimum(yf, 0.0) - jnp.log1p(jnp.exp(-jnp.abs(yf)))
    lane = lax.broadcasted_iota(jnp.int32, lf.shape, 1)
    n_heads = qf_ref.shape[2] // LANES
    lf = jnp.where(lane < n_heads, lf, 0.0)
    c = _dot(tri_ref[...], jnp.concatenate(_split3(lf), axis=1))
    f_cum = c[:, :LANES] + c[:, LANES:2 * LANES] + c[:, 2 * LANES:] + carry_scr[...]
    carry_scr[...] = f_cum[-1:, :]
    f_bits = f_cum * LOG2E
    fparts = jnp.concatenate(_split3(f_bits), axis=1)
    t = fvt_ref.shape[3]
    fend_ref[0, 0] = jnp.zeros(fend_ref.shape[2:], F32)
    for c in range(fvt_ref.shape[1]):
        fend_ref[0, 0, c:c + 1, :] = f_bits[(c + 1) * t - 1:(c + 1) * t, :]

    def headnorm(y, gain):
        hi, lo = _split2(y * y)
        ms = _dot(hi, sel_ref[...]) + _dot(lo, sel_ref[...])
        rhi, rlo = _split2(lax.rsqrt(ms + EPS))
        return y * (_dot(rhi, selt_ref[...]) + _dot(rlo, selt_ref[...])) * gain

    q = headnorm(_dot(h, wq_ref[...]), gq_ref[...])
    qf_ref[0] = (q + _dot(fparts, pq_ref[...]) + cq_ref[...]).astype(BF16)
    k = headnorm(_dot(h, wk_ref[...]), gk_ref[...])
    kf_ref[0] = (k + _dot(fparts, pk_ref[...]) + ck_ref[...]).astype(BF16)

    fvt =_dot_nt(wvt_ref[...], h).astype(BF16)
    svt = _dot_nt(wsvt_ref[...], h).astype(BF16)
    for c in range(fvt_ref.shape[1]):
        fvt_ref[0, c] = fvt[:, c * t:(c + 1) * t]
        svt_ref[0, c] = svt[:, c * t:(c + 1) * t]

    width = sq_ref.shape[2]
    ysb = _dot(h, wsqk_ref[...])
    sq_ref[0] = (ysb[:, :width] * (LOG2E * HEAD_DIM ** -0.5)).astype(BF16)
    sk_ref[0] = ysb[:, width:].astype(BF16)

    d = gf_ref.shape[2]
    yg = _sigmoid(_dot(h, wgate_ref[...]))
    gf_ref[0] = yg[:, :d].astype(BF16)
    gs_ref[0] = yg[:, d:].astype(BF16)


def _inproj(x3d, consts):
    b, s, d = x3d.shape
    tm = TOKEN_TILE
    t = ATTN_TILE
    spread = consts[1].shape[1]
    width = consts[3].shape[0]
    ins = [x3d] + list(consts)
    in_specs = [pl.BlockSpec((1, tm, d), lambda bi, i: (bi, i, 0))]
    in_specs += [_const_spec(c.shape) for c in consts]

    def tok(n):
        return pl.BlockSpec((1, tm, n), lambda bi, i: (bi, i, 0))

    def shp(n):
        return jax.ShapeDtypeStruct((b, s, n), BF16)

    vt_spec = pl.BlockSpec((1, tm // t, width, t), lambda bi, i: (bi, i, 0, 0))
    vt_shape = jax.ShapeDtypeStruct((b, s // t, width, t), BF16)
    fend_spec = pl.BlockSpec((1, 1, SUBLANES, LANES), lambda bi, i: (bi, i, 0, 0))
    fend_shape = jax.ShapeDtypeStruct((b, s // tm, SUBLANES, LANES), F32)

    return pl.pallas_call(
        _inproj_kernel,
        grid=(b, s // tm),
        in_specs=in_specs,
        out_specs=[tok(spread), tok(spread), vt_spec, tok(width), tok(width), vt_spec,
                   tok(d), tok(d), fend_spec],
        out_shape=[shp(spread), shp(spread), vt_shape, shp(width), shp(width), vt_shape,
                   shp(d), shp(d), fend_shape],
        scratch_shapes=[pltpu.VMEM((1, LANES), F32)],
        compiler_params=_params("parallel", "arbitrary"),
        name="inproj",
    )(*ins)


HEADS_PER_STEP = 8
PAIRS_PER_STEP = HEADS_PER_STEP // 2


def _key_query_iota(t):
    return (lax.broadcasted_iota(jnp.int32, (t, t), 0), lax.broadcasted_iota(jnp.int32, (t, t), 1))


def _store_heads(o_ref, outs):
    sub = lax.broadcasted_iota(jnp.int32, outs[0].shape, 0)
    for pr in range(PAIRS_PER_STEP):
        pair = jnp.where(sub < HEAD_DIM, outs[2 * pr], outs[2 * pr + 1])
        o_ref[0, :, pr * LANES:(pr + 1) * LANES] = pair.T.astype(o_ref.dtype)


def _fox_kernel(fend_ref, margin_ref, qf_ref, kf_ref, vt_ref, o_ref, m_scr, l_scr, acc_scr):
    bi = pl.program_id(0)
    hg = pl.program_id(1)
    i = pl.program_id(2)
    t = qf_ref.shape[1]
    heads = range(HEADS_PER_STEP)

    def tile_needed(j):
        need = False
        for h in heads:
            row = (bi * pl.num_programs(1) + hg) * HEADS_PER_STEP + h
            gap = fend_ref[row, j] - fend_ref[row, jnp.maximum(i - 1, 0)]
            need = jnp.logical_or(need, gap <= margin_ref[0])
        return need

    first = lax.while_loop(
        lambda j: jnp.logical_and(j > 0, tile_needed(jnp.maximum(j - 1, 0))), lambda j: j - 1, i)

    key, qry = _key_query_iota(t)
    causal = key <= qry
    q = [qf_ref[0, :, h * LANES:(h + 1) * LANES] for h in heads]
    m_scr[...] = jnp.full_like(m_scr, -jnp.inf)
    l_scr[...] = jnp.zeros_like(l_scr)
    acc_scr[...] = jnp.zeros_like(acc_scr)

    def block(j, diagonal):
        start = pl.multiple_of(j * t, t)
        s = [_dot_nt(kf_ref[0, pl.ds(start, t), h * LANES:(h + 1) * LANES], q[h])
             for h in heads]
        if diagonal:
            s = [jnp.where(causal, sh, -jnp.inf) for sh in s]
        m_old = [m_scr[h] for h in heads]
        m_new = [jnp.maximum(m_old[h], jnp.max(s[h], axis=0, keepdims=True)) for h in heads]
        alpha = [jnp.exp2(m_old[h] - m_new[h]) for h in heads]
        p = [jnp.exp2(s[h] - m_new[h]) for h in heads]
        for h in heads:
            l_scr[h] = alpha[h] * l_scr[h] + jnp.sum(p[h], axis=0, keepdims=True)
            m_scr[h] = m_new[h]
        pv = [_dot(vt_ref[0, j, (h // 2) * LANES:(h // 2 + 1) * LANES, :], p[h].astype(BF16))
              for h in heads]
        for h in heads:
            acc_scr[h] = alpha[h] * acc_scr[h] + pv[h]

    def body(j, carry):
        block(j, False)
        return carry

    lax.fori_loop(first, i, body, 0)
    block(i, True)
    _store_heads(o_ref, [acc_scr[h] * (1.0 / l_scr[h]) for h in heads])


def _attn_scratch(t):
    n = HEADS_PER_STEP
    return [pltpu.VMEM((n, 1, t), F32), pltpu.VMEM((n, 1, t), F32), pltpu.VMEM((n, LANES, t), F32)]


def _fox(fend, margin, qf, kf, fvt):
    b, nblk, width, t = fvt.shape
    s = nblk * t
    vw = PAIRS_PER_STEP * LANES
    qw = HEADS_PER_STEP * LANES
    smem = pl.BlockSpec(memory_space=pltpu.SMEM)
    return pl.pallas_call(
        _fox_kernel,
        grid=(b, width // vw, nblk),
        in_specs=[
            smem, smem,
            pl.BlockSpec((1, t, qw), lambda bi, hg, i: (bi, i, hg)),
            pl.BlockSpec((1, s, qw), lambda bi, hg, i: (bi, 0, hg)),
            pl.BlockSpec((1, nblk, vw, t), lambda bi, hg, i: (bi, 0, hg, 0)),
        ],
        out_specs=pl.BlockSpec((1, t, vw), lambda bi, hg, i: (bi, i, hg)),
        out_shape=jax.ShapeDtypeStruct((b, s, width), BF16),
        scratch_shapes=_attn_scratch(t),
        compiler_params=_params("parallel", "parallel", "arbitrary"),
        name="fox_attention",
    )(fend, margin, qf, kf, fvt)


def _sb_kernel(q_ref, k_ref, vt_ref, u_ref, o_ref, r_scr, acc_scr):
    i = pl.program_id(2)
    t = q_ref.shape[1]
    heads = range(HEADS_PER_STEP)
    key, qry = _key_query_iota(t)
    strict = key < qry
    qlane = lax.broadcasted_iota(jnp.int32, (t, LANES), 1)
    q = []
    for h in heads:
        qpair = q_ref[0, :, (h // 2) * LANES:(h // 2 + 1) * LANES]
        q.append(jnp.where((qlane < HEAD_DIM) == (h % 2 == 0), qpair, jnp.zeros_like(qpair)))
    r_scr[...] = jnp.zeros_like(r_scr)
    acc_scr[...] = jnp.zeros_like(acc_scr)

    def block(j, diagonal):
        start = pl.multiple_of(j * t, t)
        z = [_dot_nt(k_ref[0, pl.ds(start, t), (h // 2) * LANES:(h // 2 + 1) * LANES], q[h])
             for h in heads]
        sp = [jnp.maximum(zh, 0.0) + jnp.log2(1.0 + jnp.exp2(-jnp.abs(zh))) for zh in z]
        log_beta = [z[h] - sp[h] for h in heads]
        if diagonal:
            sp = [jnp.where(strict, sh, 0.0) for sh in sp]
        after = [_dot(u_ref[...], sh.astype(BF16)) for sh in sp]
        a = [jnp.exp2(log_beta[h] - after[h]) for h in heads]
        if diagonal:
            a = [jnp.where(strict, ah, 0.0) for ah in a]
        pv = [_dot(vt_ref[0, j, (h // 2) * LANES:(h // 2 + 1) * LANES, :], a[h].astype(BF16))
              for h in heads]
        for h in heads:
            r = r_scr[h]
            acc_scr[h] += jnp.exp2(-r) * pv[h]
            r_scr[h] = r + jnp.sum(sp[h], axis=0, keepdims=True)

    def least_r():
        r = r_scr[0]
        for h in heads[1:]:
            r = jnp.minimum(r, r_scr[h])
        return jnp.min(r)

    def body(carry):
        n, _ = carry
        block(i - 1 - n, False)
        return n + 1, least_r()

    block(i, True)
    lax.while_loop(lambda c: jnp.logical_and(c[0] < i, c[1] < ZERO_WEIGHT_BITS), body,
                   (jnp.int32(0), least_r()))
    _store_heads(o_ref, [acc_scr[h] for h in heads])


def _sb(sq, sk, svt, later_mat):
    b, nblk, width, t = svt.shape
    s = nblk * t
    vw = PAIRS_PER_STEP * LANES
    return pl.pallas_call(
        _sb_kernel,
        grid=(b, width // vw, nblk),
        in_specs=[
            pl.BlockSpec((1, t, vw), lambda bi, hg, i: (bi, i, hg)),
            pl.BlockSpec((1, s, vw), lambda bi, hg, i: (bi, 0, hg)),
            pl.BlockSpec((1, nblk, vw, t), lambda bi, hg, i: (bi, 0, hg, 0)),
            _const_spec(later_mat.shape),
        ],
        out_specs=pl.BlockSpec((1, t, vw), lambda bi, hg, i: (bi, i, hg)),
        out_shape=jax.ShapeDtypeStruct((b, s, width), BF16),
        scratch_shapes=_attn_scratch(t)[1:],
        compiler_params=_params("parallel", "parallel", "arbitrary"),
        name="sb_attention",
    )(sq, sk, svt, later_mat)


def _mixout_kernel(x_ref, yf_ref, ys_ref, gf_ref, gs_ref, wbf_ref, wbs_ref, wo_ref, o_ref):
    merged = (gf_ref[...].astype(F32) * _dot(yf_ref[...], wbf_ref[...])
              + gs_ref[...].astype(F32) * _dot(ys_ref[...], wbs_ref[...]))
    o_ref[...] = x_ref[...] + _dot(merged.astype(BF16), wo_ref[...])


def _mixout(x2d, yf, ys, gf, gs, wbf, wbs, wo):
    n, d = x2d.shape
    w = yf.shape[1]
    tm = TOKEN_TILE

    def tok(c):
        return pl.BlockSpec((tm, c), lambda i: (i, 0))

    return pl.pallas_call(
        _mixout_kernel,
        grid=(n // tm,),
        in_specs=[tok(d), tok(w), tok(w), tok(d), tok(d),
                  _const_spec(wbf.shape), _const_spec(wbs.shape), _const_spec(wo.shape)],
        out_specs=tok(d),
        out_shape=jax.ShapeDtypeStruct((n, d), F32),
        compiler_params=_params("parallel"),
        name="mixout",
    )(x2d, yf, ys, gf, gs, wbf, wbs, wo)


def _ple_kernel(x_ref, p_ref, g_ref, wg_ref, wp_ref, o_ref):
    x = x_ref[...]
    h = _rms_rows(x, g_ref[...]).astype(BF16)
    gate = _sigmoid(_dot(h, wg_ref[...]))
    o_ref[...] = x + gate * _dot(p_ref[...].astype(BF16), wp_ref[...])


def _ple(x2d, p2d, g, wg, wp):
    n, d = x2d.shape
    pd = p2d.shape[1]
    tm = TOKEN_TILE
    return pl.pallas_call(
        _ple_kernel,
        grid=(n // tm,),
        in_specs=[pl.BlockSpec((tm, d), lambda i: (i, 0)), pl.BlockSpec((tm, pd), lambda i: (i, 0)),
                  _const_spec(g.shape), _const_spec(wg.shape), _const_spec(wp.shape)],
        out_specs=pl.BlockSpec((tm, d), lambda i: (i, 0)),
        out_shape=jax.ShapeDtypeStruct((n, d), F32),
        compiler_params=_params("parallel"),
        name="ple",
    )(x2d, p2d, g, wg, wp)


@functools.lru_cache(maxsize=None)
def _layout_constants(n_heads, tile):
    spread = n_heads * LANES
    sel = np.zeros((spread, LANES), np.float32)
    pq = np.zeros((3 * LANES, spread), np.float32)
    pk = np.zeros((3 * LANES, spread), np.float32)
    cq = np.zeros((1, spread), np.float32)
    ck = np.zeros((1, spread), np.float32)
    for h in range(n_heads):
        base = h * LANES
        sel[base:base + HEAD_DIM, h] = 1.0 / HEAD_DIM
        for part in range(3):
            pq[part * LANES + h, base + HEAD_DIM + part] = 1.0
            pk[part * LANES + h, base + HEAD_DIM + 3 + part] = -1.0
            cq[0, base + HEAD_DIM + 3 + part] = 1.0
            ck[0, base + HEAD_DIM + part] = 1.0
    selt = (sel.T > 0).astype(np.float32)
    tri = np.tril(np.ones((tile, tile), np.float32))
    return sel, selt, tri, pq, pk, cq, ck


@functools.lru_cache(maxsize=None)
def _later_matrix(tile):
    return np.triu(np.ones((tile, tile), np.float32), k=1)


def _spread_cols(w, n_heads):
    d = w.shape[0]
    w = w.reshape(d, n_heads, HEAD_DIM)
    w = jnp.pad(w, ((0, 0), (0, 0), (0, LANES - HEAD_DIM)))
    return w.reshape(d, n_heads * LANES)


def kernel(x, p, ffn1_norm, ffn1_w_gate, ffn1_w_up, ffn1_w_down, mix_norm, w_in, forget_bias, q_norm, k_norm, w_branch_fox, w_branch_sb, w_out, ffn2_norm, ffn2_w_gate, ffn2_w_up, ffn2_w_down, ple_norm, w_ple_gate, w_ple_proj):
    b, s, d = x.shape
    depth = w_in.shape[0]
    fox_w = w_branch_fox.shape[1]
    sb_w = w_branch_sb.shape[1]
    n_heads = forget_bias.shape[1]
    assert fox_w == n_heads * HEAD_DIM and sb_w == fox_w and n_heads <= LANES
    assert s % TOKEN_TILE == 0 and TOKEN_TILE % ATTN_TILE == 0 and fox_w % LANES == 0

    sel, selt, tri, pq, pk, cq, ck = _layout_constants(n_heads, TOKEN_TILE)
    later = jnp.asarray(_later_matrix(ATTN_TILE), BF16)

    xf = x.reshape(b * s, d)
    for i in range(depth):
        xf = _ffn(xf, ffn1_norm[i][None], ffn1_w_gate[i].astype(BF16),
                  ffn1_w_up[i].astype(BF16), ffn1_w_down[i].astype(BF16))

        w = w_in[i]
        o = 0
        wq = _spread_cols(w[:, o:o + fox_w], n_heads); o += fox_w
        wk = _spread_cols(w[:, o:o + fox_w], n_heads); o += fox_w
        wvt = w[:, o:o + fox_w].T; o += fox_w
        wf = jnp.pad(w[:, o:o + n_heads], ((0, 0), (0, LANES - n_heads))); o += n_heads
        wsqk = w[:, o:o + 2 * sb_w]; o += 2 * sb_w
        wsvt = w[:, o:o + sb_w].T; o += sb_w
        wgate = w[:, o:o + 2 * d]
        fbias = jnp.pad(forget_bias[i][None], ((0, 0), (0, LANES - n_heads)))
        gq = _spread_cols(q_norm[i][None].repeat(n_heads, 0).reshape(1, fox_w), n_heads)
        gk = _spread_cols(k_norm[i][None].repeat(n_heads, 0).reshape(1, fox_w), n_heads)
        gq = gq * (LOG2E * HEAD_DIM ** -0.5)
        consts = [mix_norm[i][None], wq.astype(BF16), wk.astype(BF16), wvt.astype(BF16),
                  wsqk.astype(BF16), wsvt.astype(BF16), wgate.astype(BF16), wf.astype(BF16),
                  fbias, gq, gk,
                  jnp.asarray(sel, BF16), jnp.asarray(selt, BF16), jnp.asarray(tri, BF16),
                  jnp.asarray(pq, BF16), jnp.asarray(pk, BF16), jnp.asarray(cq), jnp.asarray(ck)]
        qf, kf, fvt, sq, sk, svt, gf, gs, fend = _inproj(xf.reshape(b, s, d), consts)

        fend = fend[:, :, :TOKEN_TILE // ATTN_TILE, :n_heads].reshape(b, s // ATTN_TILE, n_heads)
        fend = fend.transpose(0, 2, 1).reshape(b * n_heads, s // ATTN_TILE)
        ub = LOG2E * HEAD_DIM ** 0.5 * jnp.max(jnp.abs(q_norm[i])) * jnp.max(jnp.abs(k_norm[i]))
        margin = (2.0 * ub + ZERO_WEIGHT_BITS).reshape(1).astype(F32)

        y_fox = _fox(fend, margin, qf, kf, fvt)
        y_sb = _sb(sq, sk, svt, later)

        xf = _mixout(xf, y_fox.reshape(b * s, fox_w), y_sb.reshape(b * s, sb_w),
                     gf.reshape(b * s, d), gs.reshape(b * s, d),
                     w_branch_fox[i].astype(BF16), w_branch_sb[i].astype(BF16),
                     w_out[i].astype(BF16))

        xf = _ffn(xf, ffn2_norm[i][None], ffn2_w_gate[i].astype(BF16),
                  ffn2_w_up[i].astype(BF16), ffn2_w_down[i].astype(BF16))

        xf = _ple(xf, p[i].reshape(b * s, -1), ple_norm[i][None],
                  w_ple_gate[i].astype(BF16), w_ple_proj[i].astype(BF16))
    return xf.reshape(b, s, d)
```

```python
import functools

import jax
import jax.numpy as jnp
import numpy as np
from jax import lax
from jax.experimental import pallas as pl
from jax.experimental.pallas import tpu as pltpu

F32 = jnp.float32
BF16 = jnp.bfloat16

EPS = 1e-6
HEAD_DIM = 64
LOG2E = 1.4426950408889634
LANES = 128
SUBLANES = 8
ZERO_WEIGHT_BITS = 152.0
VMEM_LIMIT_BYTES = 56 * 1024 * 1024

TOKEN_TILE = 512
ATTN_TILE = 256

NT_DIMS = (((1,), (1,)), ((), ()))


def _dot(a, b):
    return jnp.dot(a, b, preferred_element_type=F32)


def _dot_nt(a, b):
    return lax.dot_general(a, b, NT_DIMS, preferred_element_type=F32)


def _split2(a):
    hi = a.astype(BF16)
    lo = (a - hi.astype(F32)).astype(BF16)
    return hi, lo


def _split3(a):
    p1 = a.astype(BF16)
    r1 = a - p1.astype(F32)
    p2 = r1.astype(BF16)
    r2 = r1 - p2.astype(F32)
    return p1, p2, r2.astype(BF16)


def _rms_rows(x, g):
    ms = jnp.mean(x * x, axis=-1, keepdims=True)
    return x * lax.rsqrt(ms + EPS) * g


def _sigmoid(x):
    return 1.0 / (1.0 + jnp.exp(-x))


def _params(*sem):
    return pltpu.CompilerParams(dimension_semantics=sem, vmem_limit_bytes=VMEM_LIMIT_BYTES)


def _const_spec(shape):
    nd = len(shape)
    return pl.BlockSpec(shape, lambda *_: (0,) * nd, pipeline_mode=pl.Buffered(1))


def _ffn_kernel(x_ref, g_ref, wg_ref, wu_ref, wd_ref, o_ref):
    x = x_ref[...]
    h = _rms_rows(x, g_ref[...]).astype(BF16)
    a = _dot(h, wg_ref[...])
    u = _dot(h, wu_ref[...])
    act = (a * _sigmoid(a) * u).astype(BF16)
    o_ref[...] = x + 0.5 * _dot(act, wd_ref[...])


def _ffn(x2d, g, wg, wu, wd):
    n, d = x2d.shape
    tm = TOKEN_TILE
    return pl.pallas_call(
        _ffn_kernel,
        grid=(n // tm,),
        in_specs=[pl.BlockSpec((tm, d), lambda i: (i, 0)), _const_spec(g.shape),
                  _const_spec(wg.shape), _const_spec(wu.shape), _const_spec(wd.shape)],
        out_specs=pl.BlockSpec((tm, d), lambda i: (i, 0)),
        out_shape=jax.ShapeDtypeStruct((n, d), F32),
        compiler_params=_params("parallel"),
        name="ffn",
    )(x2d, g, wg, wu, wd)


def _inproj_kernel(x_ref, g_ref, wq_ref, wk_ref, wvt_ref, wsqk_ref, wsvt_ref, wgate_ref, wf_ref,
                   fbias_ref, gq_ref, gk_ref, sel_ref, selt_ref, tri_ref,
                   pq_ref, pk_ref, cq_ref, ck_ref,
                   qf_ref, kf_ref, fvt_ref, sq_ref, sk_ref, svt_ref, gf_ref, gs_ref, fend_ref,
                   carry_scr):
    @pl.when(pl.program_id(1) == 0)
    def _():
        carry_scr[...] = jnp.zeros_like(carry_scr)

    h = _rms_rows(x_ref[0], g_ref[...]).astype(BF16)

    yf = _dot(h, wf_ref[...]) + fbias_ref[...]
    lf = jnp.minimum(yf, 0.0) - jnp.log1p(jnp.exp(-jnp.abs(yf)))
    lane = lax.broadcasted_iota(jnp.int32, lf.shape, 1)
    n_heads = qf_ref.shape[2] // LANES
    lf = jnp.where(lane < n_heads, lf, 0.0)
    c = _dot(tri_ref[...], jnp.concatenate(_split3(lf), axis=1))
    f_cum = c[:, :LANES] + c[:, LANES:2 * LANES] + c[:, 2 * LANES:] + carry_scr[...]
    carry_scr[...] = f_cum[-1:, :]
    f_bits = f_cum * LOG2E
    fparts = jnp.concatenate(_split3(f_bits), axis=1)
    t = fvt_ref.shape[3]
    fend_ref[0, 0] = jnp.zeros(fend_ref.shape[2:], F32)
    for c in range(fvt_ref.shape[1]):
        fend_ref[0, 0, c:c + 1, :] = f_bits[(c + 1) * t - 1:(c + 1) * t, :]

    def headnorm(y, gain):
        hi, lo = _split2(y * y)
        ms = _dot(hi, sel_ref[...]) + _dot(lo, sel_ref[...])
        rhi, rlo = _split2(lax.rsqrt(ms + EPS))
        return y * (_dot(rhi, selt_ref[...]) + _dot(rlo, selt_ref[...])) * gain

    q = headnorm(_dot(h, wq_ref[...]), gq_ref[...]).astype(BF16)
    k = headnorm(_dot(h, wk_ref[...]), gk_ref[...]).astype(BF16)
    featq = (_dot(fparts, pq_ref[...]) + cq_ref[...]).astype(BF16)
    featk = (_dot(fparts, pk_ref[...]) + ck_ref[...]).astype(BF16)
    for pr in range(q.shape[1] // LANES):
        src = slice(pr * LANES, (pr + 1) * LANES)
        qf_ref[0, :, 2 * pr * LANES:(2 * pr + 1) * LANES] = q[:, src]
        qf_ref[0, :, (2 * pr + 1) * LANES:(2 * pr + 2) * LANES] = featq[:, src]
        kf_ref[0, :, 2 * pr * LANES:(2 * pr + 1) * LANES] = k[:, src]
        kf_ref[0, :, (2 * pr + 1) * LANES:(2 * pr + 2) * LANES] = featk[:, src]

    fvt =_dot_nt(wvt_ref[...], h).astype(BF16)
    svt = _dot_nt(wsvt_ref[...], h).astype(BF16)
    for c in range(fvt_ref.shape[1]):
        fvt_ref[0, c] = fvt[:, c * t:(c + 1) * t]
        svt_ref[0, c] = svt[:, c * t:(c + 1) * t]

    width = sq_ref.shape[2]
    ysb = _dot(h, wsqk_ref[...])
    sq_ref[0] = (ysb[:, :width] * (LOG2E * HEAD_DIM ** -0.5)).astype(BF16)
    sk_ref[0] = ysb[:, width:].astype(BF16)

    d = gf_ref.shape[2]
    yg = _sigmoid(_dot(h, wgate_ref[...]))
    gf_ref[0] = yg[:, :d].astype(BF16)
    gs_ref[0] = yg[:, d:].astype(BF16)


def _inproj(x3d, consts):
    b, s, d = x3d.shape
    tm = TOKEN_TILE
    t = ATTN_TILE
    width = consts[3].shape[0]
    spread = 2 * width
    ins = [x3d] + list(consts)
    in_specs = [pl.BlockSpec((1, tm, d), lambda bi, i: (bi, i, 0))]
    in_specs += [_const_spec(c.shape) for c in consts]

    def tok(n):
        return pl.BlockSpec((1, tm, n), lambda bi, i: (bi, i, 0))

    def shp(n):
        return jax.ShapeDtypeStruct((b, s, n), BF16)

    vt_spec = pl.BlockSpec((1, tm // t, width, t), lambda bi, i: (bi, i, 0, 0))
    vt_shape = jax.ShapeDtypeStruct((b, s // t, width, t), BF16)
    fend_spec = pl.BlockSpec((1, 1, SUBLANES, LANES), lambda bi, i: (bi, i, 0, 0))
    fend_shape = jax.ShapeDtypeStruct((b, s // tm, SUBLANES, LANES), F32)

    return pl.pallas_call(
        _inproj_kernel,
        grid=(b, s // tm),
        in_specs=in_specs,
        out_specs=[tok(spread), tok(spread), vt_spec, tok(width), tok(width), vt_spec,
                   tok(d), tok(d), fend_spec],
        out_shape=[shp(spread), shp(spread), vt_shape, shp(width), shp(width), vt_shape,
                   shp(d), shp(d), fend_shape],
        scratch_shapes=[pltpu.VMEM((1, LANES), F32)],
        compiler_params=_params("parallel", "arbitrary"),
        name="inproj",
    )(*ins)


HEADS_PER_STEP = 8
PAIRS_PER_STEP = HEADS_PER_STEP // 2
N_FEATURES = 6


def _key_query_iota(t):
    return (lax.broadcasted_iota(jnp.int32, (t, t), 0), lax.broadcasted_iota(jnp.int32, (t, t), 1))


def _store_heads(o_ref, outs):
    sub = lax.broadcasted_iota(jnp.int32, outs[0].shape, 0)
    for pr in range(PAIRS_PER_STEP):
        pair = jnp.where(sub < HEAD_DIM, outs[2 * pr], outs[2 * pr + 1])
        o_ref[0, :, pr * LANES:(pr + 1) * LANES] = pair.T.astype(o_ref.dtype)


def _fox_kernel(fend_ref, margin_ref, qf_ref, kf_ref, vt_ref, o_ref, m_scr, l_scr, acc_scr):
    bi = pl.program_id(0)
    hg = pl.program_id(1)
    i = pl.program_id(2)
    t = qf_ref.shape[1]
    heads = range(HEADS_PER_STEP)

    def tile_needed(j):
        need = False
        for h in heads:
            row = (bi * pl.num_programs(1) + hg) * HEADS_PER_STEP + h
            gap = fend_ref[row, j] - fend_ref[row, jnp.maximum(i - 1, 0)]
            need = jnp.logical_or(need, gap <= margin_ref[0])
        return need

    first = lax.while_loop(
        lambda j: jnp.logical_and(j > 0, tile_needed(jnp.maximum(j - 1, 0))), lambda j: j - 1, i)

    key, qry = _key_query_iota(t)
    causal = key <= qry
    lane = lax.broadcasted_iota(jnp.int32, (t, 2 * LANES), 1)
    q = []
    for h in heads:
        qpair = qf_ref[0, :, (h // 2) * 2 * LANES:(h // 2 + 1) * 2 * LANES]
        q_lo, f_lo = (h % 2) * HEAD_DIM, LANES + (h % 2) * N_FEATURES
        own = jnp.logical_or(jnp.logical_and(lane >= q_lo, lane < q_lo + HEAD_DIM),
                             jnp.logical_and(lane >= f_lo, lane < f_lo + N_FEATURES))
        q.append(jnp.where(own, qpair, jnp.zeros_like(qpair)))
    m_scr[...] = jnp.full_like(m_scr, -jnp.inf)
    l_scr[...] = jnp.zeros_like(l_scr)
    acc_scr[...] = jnp.zeros_like(acc_scr)

    def block(j, diagonal):
        start = pl.multiple_of(j * t, t)
        s = [_dot_nt(kf_ref[0, pl.ds(start, t), (h // 2) * 2 * LANES:(h // 2 + 1) * 2 * LANES], q[h])
             for h in heads]
        if diagonal:
            s = [jnp.where(causal, sh, -jnp.inf) for sh in s]
        m_old = [m_scr[h] for h in heads]
        m_new = [jnp.maximum(m_old[h], jnp.max(s[h], axis=0, keepdims=True)) for h in heads]
        alpha = [jnp.exp2(m_old[h] - m_new[h]) for h in heads]
        p = [jnp.exp2(s[h] - m_new[h]) for h in heads]
        for h in heads:
            l_scr[h] = alpha[h] * l_scr[h] + jnp.sum(p[h], axis=0, keepdims=True)
            m_scr[h] = m_new[h]
        pv = [_dot(vt_ref[0, j, (h // 2) * LANES:(h // 2 + 1) * LANES, :], p[h].astype(BF16))
              for h in heads]
        for h in heads:
            acc_scr[h] = alpha[h] * acc_scr[h] + pv[h]

    def body(j, carry):
        block(j, False)
        return carry

    lax.fori_loop(first, i, body, 0)
    block(i, True)
    _store_heads(o_ref, [acc_scr[h] * (1.0 / l_scr[h]) for h in heads])


def _attn_scratch(t):
    n = HEADS_PER_STEP
    return [pltpu.VMEM((n, 1, t), F32), pltpu.VMEM((n, 1, t), F32), pltpu.VMEM((n, LANES, t), F32)]


def _fox(fend, margin, qf, kf, fvt):
    b, nblk, width, t = fvt.shape
    s = nblk * t
    vw = PAIRS_PER_STEP * LANES
    qw = HEADS_PER_STEP * LANES
    smem = pl.BlockSpec(memory_space=pltpu.SMEM)
    return pl.pallas_call(
        _fox_kernel,
        grid=(b, width // vw, nblk),
        in_specs=[
            smem, smem,
            pl.BlockSpec((1, t, qw), lambda bi, hg, i: (bi, i, hg)),
            pl.BlockSpec((1, s, qw), lambda bi, hg, i: (bi, 0, hg)),
            pl.BlockSpec((1, nblk, vw, t), lambda bi, hg, i: (bi, 0, hg, 0)),
        ],
        out_specs=pl.BlockSpec((1, t, vw), lambda bi, hg, i: (bi, i, hg)),
        out_shape=jax.ShapeDtypeStruct((b, s, width), BF16),
        scratch_shapes=_attn_scratch(t),
        compiler_params=_params("parallel", "parallel", "arbitrary"),
        name="fox_attention",
    )(fend, margin, qf, kf, fvt)


def _sb_kernel(q_ref, k_ref, vt_ref, u_ref, o_ref, r_scr, acc_scr):
    i = pl.program_id(2)
    t = q_ref.shape[1]
    heads = range(HEADS_PER_STEP)
    key, qry = _key_query_iota(t)
    strict = key < qry
    qlane = lax.broadcasted_iota(jnp.int32, (t, LANES), 1)
    q = []
    for h in heads:
        qpair = q_ref[0, :, (h // 2) * LANES:(h // 2 + 1) * LANES]
        q.append(jnp.where((qlane < HEAD_DIM) == (h % 2 == 0), qpair, jnp.zeros_like(qpair)))
    r_scr[...] = jnp.zeros_like(r_scr)
    acc_scr[...] = jnp.zeros_like(acc_scr)

    def block(j, diagonal):
        start = pl.multiple_of(j * t, t)
        z = [_dot_nt(k_ref[0, pl.ds(start, t), (h // 2) * LANES:(h // 2 + 1) * LANES], q[h])
             for h in heads]
        sp = [jnp.maximum(zh, 0.0) + jnp.log2(1.0 + jnp.exp2(-jnp.abs(zh))) for zh in z]
        log_beta = [z[h] - sp[h] for h in heads]
        if diagonal:
            sp = [jnp.where(strict, sh, 0.0) for sh in sp]
        after = [_dot(u_ref[...], sh.astype(BF16)) for sh in sp]
        a = [jnp.exp2(log_beta[h] - after[h]) for h in heads]
        if diagonal:
            a = [jnp.where(strict, ah, 0.0) for ah in a]
        pv = [_dot(vt_ref[0, j, (h // 2) * LANES:(h // 2 + 1) * LANES, :], a[h].astype(BF16))
              for h in heads]
        for h in heads:
            r = r_scr[h]
            acc_scr[h] += jnp.exp2(-r) * pv[h]
            r_scr[h] = r + jnp.sum(sp[h], axis=0, keepdims=True)

    def least_r():
        r = r_scr[0]
        for h in heads[1:]:
            r = jnp.minimum(r, r_scr[h])
        return jnp.min(r)

    def body(carry):
        n, _ = carry
        block(i - 1 - n, False)
        return n + 1, least_r()

    block(i, True)
    lax.while_loop(lambda c: jnp.logical_and(c[0] < i, c[1] < ZERO_WEIGHT_BITS), body,
                   (jnp.int32(0), least_r()))
    _store_heads(o_ref, [acc_scr[h] for h in heads])


def _sb(sq, sk, svt, later_mat):
    b, nblk, width, t = svt.shape
    s = nblk * t
    vw = PAIRS_PER_STEP * LANES
    return pl.pallas_call(
        _sb_kernel,
        grid=(b, width // vw, nblk),
        in_specs=[
            pl.BlockSpec((1, t, vw), lambda bi, hg, i: (bi, i, hg)),
            pl.BlockSpec((1, s, vw), lambda bi, hg, i: (bi, 0, hg)),
            pl.BlockSpec((1, nblk, vw, t), lambda bi, hg, i: (bi, 0, hg, 0)),
            _const_spec(later_mat.shape),
        ],
        out_specs=pl.BlockSpec((1, t, vw), lambda bi, hg, i: (bi, i, hg)),
        out_shape=jax.ShapeDtypeStruct((b, s, width), BF16),
        scratch_shapes=_attn_scratch(t)[1:],
        compiler_params=_params("parallel", "parallel", "arbitrary"),
        name="sb_attention",
    )(sq, sk, svt, later_mat)


def _mixout_kernel(x_ref, yf_ref, ys_ref, gf_ref, gs_ref, wbf_ref, wbs_ref, wo_ref, o_ref):
    merged = (gf_ref[...].astype(F32) * _dot(yf_ref[...], wbf_ref[...])
              + gs_ref[...].astype(F32) * _dot(ys_ref[...], wbs_ref[...]))
    o_ref[...] = x_ref[...] + _dot(merged.astype(BF16), wo_ref[...])


def _mixout(x2d, yf, ys, gf, gs, wbf, wbs, wo):
    n, d = x2d.shape
    w = yf.shape[1]
    tm = TOKEN_TILE

    def tok(c):
        return pl.BlockSpec((tm, c), lambda i: (i, 0))

    return pl.pallas_call(
        _mixout_kernel,
        grid=(n // tm,),
        in_specs=[tok(d), tok(w), tok(w), tok(d), tok(d),
                  _const_spec(wbf.shape), _const_spec(wbs.shape), _const_spec(wo.shape)],
        out_specs=tok(d),
        out_shape=jax.ShapeDtypeStruct((n, d), F32),
        compiler_params=_params("parallel"),
        name="mixout",
    )(x2d, yf, ys, gf, gs, wbf, wbs, wo)


def _ple_kernel(x_ref, p_ref, g_ref, wg_ref, wp_ref, o_ref):
    x = x_ref[...]
    h = _rms_rows(x, g_ref[...]).astype(BF16)
    gate = _sigmoid(_dot(h, wg_ref[...]))
    o_ref[...] = x + gate * _dot(p_ref[...].astype(BF16), wp_ref[...])


def _ple(x2d, p2d, g, wg, wp):
    n, d = x2d.shape
    pd = p2d.shape[1]
    tm = TOKEN_TILE
    return pl.pallas_call(
        _ple_kernel,
        grid=(n // tm,),
        in_specs=[pl.BlockSpec((tm, d), lambda i: (i, 0)), pl.BlockSpec((tm, pd), lambda i: (i, 0)),
                  _const_spec(g.shape), _const_spec(wg.shape), _const_spec(wp.shape)],
        out_specs=pl.BlockSpec((tm, d), lambda i: (i, 0)),
        out_shape=jax.ShapeDtypeStruct((n, d), F32),
        compiler_params=_params("parallel"),
        name="ple",
    )(x2d, p2d, g, wg, wp)


@functools.lru_cache(maxsize=None)
def _layout_constants(n_heads, tile):
    width = n_heads * HEAD_DIM
    sel = np.zeros((width, LANES), np.float32)
    pq = np.zeros((3 * LANES, width), np.float32)
    pk = np.zeros((3 * LANES, width), np.float32)
    cq = np.zeros((1, width), np.float32)
    ck = np.zeros((1, width), np.float32)
    for h in range(n_heads):
        sel[h * HEAD_DIM:(h + 1) * HEAD_DIM, h] = 1.0 / HEAD_DIM
        base = (h // 2) * LANES + (h % 2) * N_FEATURES
        for part in range(3):
            pq[part * LANES + h, base + part] = 1.0
            pk[part * LANES + h, base + 3 + part] = -1.0
            cq[0, base + 3 + part] = 1.0
            ck[0, base + part] = 1.0
    selt = (sel.T > 0).astype(np.float32)
    tri = np.tril(np.ones((tile, tile), np.float32))
    return sel, selt, tri, pq, pk, cq, ck


@functools.lru_cache(maxsize=None)
def _later_matrix(tile):
    return np.triu(np.ones((tile, tile), np.float32), k=1)


def kernel(x, p, ffn1_norm, ffn1_w_gate, ffn1_w_up, ffn1_w_down, mix_norm, w_in, forget_bias, q_norm, k_norm, w_branch_fox, w_branch_sb, w_out, ffn2_norm, ffn2_w_gate, ffn2_w_up, ffn2_w_down, ple_norm, w_ple_gate, w_ple_proj):
    b, s, d = x.shape
    depth = w_in.shape[0]
    fox_w = w_branch_fox.shape[1]
    sb_w = w_branch_sb.shape[1]
    n_heads = forget_bias.shape[1]
    assert fox_w == n_heads * HEAD_DIM and sb_w == fox_w and n_heads <= LANES
    assert s % TOKEN_TILE == 0 and TOKEN_TILE % ATTN_TILE == 0 and TOKEN_TILE // ATTN_TILE <= SUBLANES
    assert n_heads % HEADS_PER_STEP == 0 and 2 * N_FEATURES <= LANES

    sel, selt, tri, pq, pk, cq, ck = _layout_constants(n_heads, TOKEN_TILE)
    later = jnp.asarray(_later_matrix(ATTN_TILE), BF16)

    xf = x.reshape(b * s, d)
    for i in range(depth):
        xf = _ffn(xf, ffn1_norm[i][None], ffn1_w_gate[i].astype(BF16),
                  ffn1_w_up[i].astype(BF16), ffn1_w_down[i].astype(BF16))

        w = w_in[i]
        o = 0
        wq = w[:, o:o + fox_w]; o += fox_w
        wk = w[:, o:o + fox_w]; o += fox_w
        wvt = w[:, o:o + fox_w].T; o += fox_w
        wf = jnp.pad(w[:, o:o + n_heads], ((0, 0), (0, LANES - n_heads))); o += n_heads
        wsqk = w[:, o:o + 2 * sb_w]; o += 2 * sb_w
        wsvt = w[:, o:o + sb_w].T; o += sb_w
        wgate = w[:, o:o + 2 * d]
        fbias = jnp.pad(forget_bias[i][None], ((0, 0), (0, LANES - n_heads)))
        gq = jnp.tile(q_norm[i], n_heads)[None] * (LOG2E * HEAD_DIM ** -0.5)
        gk = jnp.tile(k_norm[i], n_heads)[None]
        consts = [mix_norm[i][None], wq.astype(BF16), wk.astype(BF16), wvt.astype(BF16),
                  wsqk.astype(BF16), wsvt.astype(BF16), wgate.astype(BF16), wf.astype(BF16),
                  fbias, gq, gk,
                  jnp.asarray(sel, BF16), jnp.asarray(selt, BF16), jnp.asarray(tri, BF16),
                  jnp.asarray(pq, BF16), jnp.asarray(pk, BF16), jnp.asarray(cq), jnp.asarray(ck)]
        qf, kf, fvt, sq, sk, svt, gf, gs, fend = _inproj(xf.reshape(b, s, d), consts)

        fend = fend[:, :, :TOKEN_TILE // ATTN_TILE, :n_heads].reshape(b, s // ATTN_TILE, n_heads)
        fend = fend.transpose(0, 2, 1).reshape(b * n_heads, s // ATTN_TILE)
        ub = LOG2E * HEAD_DIM ** 0.5 * jnp.max(jnp.abs(q_norm[i])) * jnp.max(jnp.abs(k_norm[i]))
        margin = (2.0 * ub + ZERO_WEIGHT_BITS).reshape(1).astype(F32)

        y_fox = _fox(fend, margin, qf, kf, fvt)
        y_sb = _sb(sq, sk, svt, later)

        xf = _mixout(xf, y_fox.reshape(b * s, fox_w), y_sb.reshape(b * s, sb_w),
                     gf.reshape(b * s, d), gs.reshape(b * s, d),
                     w_branch_fox[i].astype(BF16), w_branch_sb[i].astype(BF16),
                     w_out[i].astype(BF16))

        xf = _ffn(xf, ffn2_norm[i][None], ffn2_w_gate[i].astype(BF16),
                  ffn2_w_up[i].astype(BF16), ffn2_w_down[i].astype(BF16))

        xf = _ple(xf, p[i].reshape(b * s, -1), ple_norm[i][None],
                  w_ple_gate[i].astype(BF16), w_ple_proj[i].astype(BF16))
    return xf.reshape(b, s, d)
```

```python
import functools

import jax
import jax.numpy as jnp
import numpy as np
from jax import lax
from jax.experimental import pallas as pl
from jax.experimental.pallas import tpu as pltpu

F32 = jnp.float32
BF16 = jnp.bfloat16

EPS = 1e-6
HEAD_DIM = 64
LOG2E = 1.4426950408889634
LANES = 128
SUBLANES = 8
ZERO_WEIGHT_BITS = 152.0
VMEM_LIMIT_BYTES = 56 * 1024 * 1024

TOKEN_TILE = 512
POST_TILE = 512
ATTN_TILE = 256

NT_DIMS = (((1,), (1,)), ((), ()))


def _dot(a, b):
    return jnp.dot(a, b, preferred_element_type=F32)


def _dot_nt(a, b):
    return lax.dot_general(a, b, NT_DIMS, preferred_element_type=F32)


def _split2(a):
    hi = a.astype(BF16)
    lo = (a - hi.astype(F32)).astype(BF16)
    return hi, lo


def _split3(a):
    p1 = a.astype(BF16)
    r1 = a - p1.astype(F32)
    p2 = r1.astype(BF16)
    r2 = r1 - p2.astype(F32)
    return p1, p2, r2.astype(BF16)


def _rms_rows(x, g):
    ms = jnp.mean(x * x, axis=-1, keepdims=True)
    return x * lax.rsqrt(ms + EPS) * g


def _sigmoid(x):
    return 1.0 / (1.0 + jnp.exp(-x))


def _params(*sem):
    return pltpu.CompilerParams(dimension_semantics=sem, vmem_limit_bytes=VMEM_LIMIT_BYTES)


def _const_spec(shape):
    nd = len(shape)
    return pl.BlockSpec(shape, lambda *_: (0,) * nd, pipeline_mode=pl.Buffered(1))


def _ffn_kernel(x_ref, g_ref, wg_ref, wu_ref, wd_ref, o_ref):
    x = x_ref[...]
    h = _rms_rows(x, g_ref[...]).astype(BF16)
    a = _dot(h, wg_ref[...])
    u = _dot(h, wu_ref[...])
    act = (a * _sigmoid(a) * u).astype(BF16)
    o_ref[...] = x + 0.5 * _dot(act, wd_ref[...])


def _ffn(x2d, g, wg, wu, wd):
    n, d = x2d.shape
    tm = TOKEN_TILE
    return pl.pallas_call(
        _ffn_kernel,
        grid=(n // tm,),
        in_specs=[pl.BlockSpec((tm, d), lambda i: (i, 0)), _const_spec(g.shape),
                  _const_spec(wg.shape), _const_spec(wu.shape), _const_spec(wd.shape)],
        out_specs=pl.BlockSpec((tm, d), lambda i: (i, 0)),
        out_shape=jax.ShapeDtypeStruct((n, d), F32),
        compiler_params=_params("parallel"),
        name="ffn",
    )(x2d, g, wg, wu, wd)


def _inproj_kernel(x_ref, g_ref, wq_ref, wk_ref, wvt_ref, wsqk_ref, wsvt_ref, wgate_ref, wf_ref,
                   fbias_ref, gq_ref, gk_ref, sel_ref, selt_ref, tri_ref,
                   pq_ref, pk_ref, cq_ref, ck_ref,
                   qf_ref, kf_ref, fvt_ref, sq_ref, sk_ref, svt_ref, gf_ref, gs_ref, fend_ref,
                   carry_scr):
    @pl.when(pl.program_id(1) == 0)
    def _():
        carry_scr[...] = jnp.zeros_like(carry_scr)

    h = _rms_rows(x_ref[0], g_ref[...]).astype(BF16)

    yf = _dot(h, wf_ref[...]) + fbias_ref[...]
    lf = jnp.minimum(yf, 0.0) - jnp.log1p(jnp.exp(-jnp.abs(yf)))
    lane = lax.broadcasted_iota(jnp.int32, lf.shape, 1)
    n_heads = qf_ref.shape[2] // LANES
    lf = jnp.where(lane < n_heads, lf, 0.0)
    c = _dot(tri_ref[...], jnp.concatenate(_split3(lf), axis=1))
    f_cum = c[:, :LANES] + c[:, LANES:2 * LANES] + c[:, 2 * LANES:] + carry_scr[...]
    carry_scr[...] = f_cum[-1:, :]
    f_bits = f_cum * LOG2E
    fparts = jnp.concatenate(_split3(f_bits), axis=1)
    t = fvt_ref.shape[3]
    fend_ref[0, 0] = jnp.zeros(fend_ref.shape[2:], F32)
    for c in range(fvt_ref.shape[1]):
        fend_ref[0, 0, c:c + 1, :] = f_bits[(c + 1) * t - 1:(c + 1) * t, :]

    def headnorm(y, gain):
        hi, lo = _split2(y * y)
        ms = _dot(hi, sel_ref[...]) + _dot(lo, sel_ref[...])
        rhi, rlo = _split2(lax.rsqrt(ms + EPS))
        return y * (_dot(rhi, selt_ref[...]) + _dot(rlo, selt_ref[...])) * gain

    q = headnorm(_dot(h, wq_ref[...]), gq_ref[...]).astype(BF16)
    k = headnorm(_dot(h, wk_ref[...]), gk_ref[...]).astype(BF16)
    featq = (_dot(fparts, pq_ref[...]) + cq_ref[...]).astype(BF16)
    featk = (_dot(fparts, pk_ref[...]) + ck_ref[...]).astype(BF16)
    for pr in range(q.shape[1] // LANES):
        src = slice(pr * LANES, (pr + 1) * LANES)
        qf_ref[0, :, 2 * pr * LANES:(2 * pr + 1) * LANES] = q[:, src]
        qf_ref[0, :, (2 * pr + 1) * LANES:(2 * pr + 2) * LANES] = featq[:, src]
        kf_ref[0, :, 2 * pr * LANES:(2 * pr + 1) * LANES] = k[:, src]
        kf_ref[0, :, (2 * pr + 1) * LANES:(2 * pr + 2) * LANES] = featk[:, src]

    fvt =_dot_nt(wvt_ref[...], h).astype(BF16)
    svt = _dot_nt(wsvt_ref[...], h).astype(BF16)
    for c in range(fvt_ref.shape[1]):
        fvt_ref[0, c] = fvt[:, c * t:(c + 1) * t]
        svt_ref[0, c] = svt[:, c * t:(c + 1) * t]

    width = sq_ref.shape[2]
    ysb = _dot(h, wsqk_ref[...])
    sq_ref[0] = (ysb[:, :width] * (LOG2E * HEAD_DIM ** -0.5)).astype(BF16)
    sk_ref[0] = ysb[:, width:].astype(BF16)

    d = gf_ref.shape[2]
    yg = _sigmoid(_dot(h, wgate_ref[...]))
    gf_ref[0] = yg[:, :d].astype(BF16)
    gs_ref[0] = yg[:, d:].astype(BF16)


def _inproj(x3d, consts):
    b, s, d = x3d.shape
    tm = TOKEN_TILE
    t = ATTN_TILE
    width = consts[3].shape[0]
    spread = 2 * width
    ins = [x3d] + list(consts)
    in_specs = [pl.BlockSpec((1, tm, d), lambda bi, i: (bi, i, 0))]
    in_specs += [_const_spec(c.shape) for c in consts]

    def tok(n):
        return pl.BlockSpec((1, tm, n), lambda bi, i: (bi, i, 0))

    def shp(n):
        return jax.ShapeDtypeStruct((b, s, n), BF16)

    vt_spec = pl.BlockSpec((1, tm // t, width, t), lambda bi, i: (bi, i, 0, 0))
    vt_shape = jax.ShapeDtypeStruct((b, s // t, width, t), BF16)
    fend_spec = pl.BlockSpec((1, 1, SUBLANES, LANES), lambda bi, i: (bi, i, 0, 0))
    fend_shape = jax.ShapeDtypeStruct((b, s // tm, SUBLANES, LANES), F32)

    return pl.pallas_call(
        _inproj_kernel,
        grid=(b, s // tm),
        in_specs=in_specs,
        out_specs=[tok(spread), tok(spread), vt_spec, tok(width), tok(width), vt_spec,
                   tok(d), tok(d), fend_spec],
        out_shape=[shp(spread), shp(spread), vt_shape, shp(width), shp(width), vt_shape,
                   shp(d), shp(d), fend_shape],
        scratch_shapes=[pltpu.VMEM((1, LANES), F32)],
        compiler_params=_params("parallel", "arbitrary"),
        name="inproj",
    )(*ins)


HEADS_PER_STEP = 8
PAIRS_PER_STEP = HEADS_PER_STEP // 2
N_FEATURES = 6


def _key_query_iota(t):
    return (lax.broadcasted_iota(jnp.int32, (t, t), 0), lax.broadcasted_iota(jnp.int32, (t, t), 1))


def _store_heads(o_ref, outs):
    sub = lax.broadcasted_iota(jnp.int32, outs[0].shape, 0)
    for pr in range(PAIRS_PER_STEP):
        pair = jnp.where(sub < HEAD_DIM, outs[2 * pr], outs[2 * pr + 1])
        o_ref[0, :, pr * LANES:(pr + 1) * LANES] = pair.T.astype(o_ref.dtype)


def _fox_kernel(fend_ref, margin_ref, qf_ref, kf_ref, vt_ref, o_ref, m_scr, l_scr, acc_scr, s_scr):
    bi = pl.program_id(0)
    hg = pl.program_id(1)
    i = pl.program_id(2)
    t = qf_ref.shape[1]
    heads = range(HEADS_PER_STEP)

    def tile_needed(j):
        need = False
        for h in heads:
            row = (bi * pl.num_programs(1) + hg) * HEADS_PER_STEP + h
            gap = fend_ref[row, j] - fend_ref[row, jnp.maximum(i - 1, 0)]
            need = jnp.logical_or(need, gap <= margin_ref[0])
        return need

    first = lax.while_loop(
        lambda j: jnp.logical_and(j > 0, tile_needed(jnp.maximum(j - 1, 0))), lambda j: j - 1, i)

    key, qry = _key_query_iota(t)
    causal = key <= qry
    lane = lax.broadcasted_iota(jnp.int32, (t, 2 * LANES), 1)
    q = []
    for h in heads:
        qpair = qf_ref[0, :, (h // 2) * 2 * LANES:(h // 2 + 1) * 2 * LANES]
        q_lo, f_lo = (h % 2) * HEAD_DIM, LANES + (h % 2) * N_FEATURES
        own = jnp.logical_or(jnp.logical_and(lane >= q_lo, lane < q_lo + HEAD_DIM),
                             jnp.logical_and(lane >= f_lo, lane < f_lo + N_FEATURES))
        q.append(jnp.where(own, qpair, jnp.zeros_like(qpair)))
    m_scr[...] = jnp.full_like(m_scr, -jnp.inf)
    l_scr[...] = jnp.zeros_like(l_scr)
    acc_scr[...] = jnp.zeros_like(acc_scr)

    ones_rows = jnp.ones((ONES_ROWS, t), BF16)

    def logits(j):
        start = pl.multiple_of(j * t, t)
        return [_dot_nt(kf_ref[0, pl.ds(start, t), (h // 2) * 2 * LANES:(h // 2 + 1) * 2 * LANES], q[h])
                for h in heads]

    def softmax_step(s, j, diagonal):
        if diagonal:
            s = [jnp.where(causal, sh, -jnp.inf) for sh in s]
        m_old = [m_scr[h] for h in heads]
        m_new = [jnp.maximum(m_old[h], jnp.max(s[h], axis=0, keepdims=True)) for h in heads]
        alpha = [jnp.exp2(m_old[h] - m_new[h]) for h in heads]
        p = [jnp.exp2(s[h] - m_new[h]).astype(BF16) for h in heads]
        vt1 = [jnp.concatenate([vt_ref[0, j, pr * LANES:(pr + 1) * LANES, :], ones_rows], axis=0)
               for pr in range(PAIRS_PER_STEP)]
        pv = [_dot(vt1[h // 2], p[h]) for h in heads]
        for h in heads:
            m_scr[h] = m_new[h]
            l_scr[h] = alpha[h] * l_scr[h] + pv[h][LANES:LANES + 1, :]
            acc_scr[h] = alpha[h] * acc_scr[h] + pv[h][:LANES, :]

    def body(j, carry):
        s_next = logits(j + 1)
        softmax_step([s_scr[h] for h in heads], j, False)
        for h in heads:
            s_scr[h] = s_next[h]
        return carry

    s_first = logits(first)
    for h in heads:
        s_scr[h] = s_first[h]
    lax.fori_loop(first, i, body, 0)
    softmax_step([s_scr[h] for h in heads], i, True)
    _store_heads(o_ref, [acc_scr[h] * (1.0 / l_scr[h]) for h in heads])


ONES_ROWS = 16


def _attn_scratch(t):
    n = HEADS_PER_STEP
    return [pltpu.VMEM((n, 1, t), F32), pltpu.VMEM((n, 1, t), F32), pltpu.VMEM((n, LANES, t), F32),
            pltpu.VMEM((n, t, t), F32)]


def _fox(fend, margin, qf, kf, fvt):
    b, nblk, width, t = fvt.shape
    s = nblk * t
    vw = PAIRS_PER_STEP * LANES
    qw = HEADS_PER_STEP * LANES
    smem = pl.BlockSpec(memory_space=pltpu.SMEM)
    return pl.pallas_call(
        _fox_kernel,
        grid=(b, width // vw, nblk),
        in_specs=[
            smem, smem,
            pl.BlockSpec((1, t, qw), lambda bi, hg, i: (bi, i, hg)),
            pl.BlockSpec((1, s, qw), lambda bi, hg, i: (bi, 0, hg)),
            pl.BlockSpec((1, nblk, vw, t), lambda bi, hg, i: (bi, 0, hg, 0)),
        ],
        out_specs=pl.BlockSpec((1, t, vw), lambda bi, hg, i: (bi, i, hg)),
        out_shape=jax.ShapeDtypeStruct((b, s, width), BF16),
        scratch_shapes=_attn_scratch(t),
        compiler_params=_params("parallel", "parallel", "arbitrary"),
        name="fox_attention",
    )(fend, margin, qf, kf, fvt)


def _sb_kernel(q_ref, k_ref, vt_ref, u_ref, o_ref, r_scr, acc_scr, z_scr):
    i = pl.program_id(2)
    t = q_ref.shape[1]
    heads = range(HEADS_PER_STEP)
    key, qry = _key_query_iota(t)
    strict = key < qry
    qlane = lax.broadcasted_iota(jnp.int32, (t, LANES), 1)
    q = []
    for h in heads:
        qpair = q_ref[0, :, (h // 2) * LANES:(h // 2 + 1) * LANES]
        q.append(jnp.where((qlane < HEAD_DIM) == (h % 2 == 0), qpair, jnp.zeros_like(qpair)))
    r_scr[...] = jnp.zeros_like(r_scr)
    acc_scr[...] = jnp.zeros_like(acc_scr)

    def logits(j):
        start = pl.multiple_of(j * t, t)
        return [_dot_nt(k_ref[0, pl.ds(start, t), (h // 2) * LANES:(h // 2 + 1) * LANES], q[h])
                for h in heads]

    def stick_step(z, j, diagonal):
        sp = [jnp.maximum(zh, 0.0) + jnp.log2(1.0 + jnp.exp2(-jnp.abs(zh))) for zh in z]
        log_beta = [z[h] - sp[h] for h in heads]
        if diagonal:
            sp = [jnp.where(strict, sh, 0.0) for sh in sp]
        sums = [_dot(u_ref[...], sh.astype(BF16)) for sh in sp]
        a = [jnp.exp2(log_beta[h] - sums[h][:t, :]) for h in heads]
        if diagonal:
            a = [jnp.where(strict, ah, 0.0) for ah in a]
        pv = [_dot(vt_ref[0, j, (h // 2) * LANES:(h // 2 + 1) * LANES, :], a[h].astype(BF16))
              for h in heads]
        for h in heads:
            r = r_scr[h]
            acc_scr[h] += jnp.exp2(-r) * pv[h]
            r_scr[h] = r + sums[h][t:t + 1, :]

    def least_r():
        r = r_scr[0]
        for h in heads[1:]:
            r = jnp.minimum(r, r_scr[h])
        return jnp.min(r)

    def visit(z, j, diagonal):
        z_next = logits(jnp.maximum(j - 1, 0))
        stick_step(z, j, diagonal)
        for h in heads:
            z_scr[h] = z_next[h]

    def body(carry):
        n, _ = carry
        visit([z_scr[h] for h in heads], i - 1 - n, False)
        return n + 1, least_r()

    visit(logits(i), i, True)
    lax.while_loop(lambda c: jnp.logical_and(c[0] < i, c[1] < ZERO_WEIGHT_BITS), body,
                   (jnp.int32(0), least_r()))
    _store_heads(o_ref, [acc_scr[h] for h in heads])


def _sb(sq, sk, svt, later_mat):
    b, nblk, width, t = svt.shape
    s = nblk * t
    vw = PAIRS_PER_STEP * LANES
    return pl.pallas_call(
        _sb_kernel,
        grid=(b, width // vw, nblk),
        in_specs=[
            pl.BlockSpec((1, t, vw), lambda bi, hg, i: (bi, i, hg)),
            pl.BlockSpec((1, s, vw), lambda bi, hg, i: (bi, 0, hg)),
            pl.BlockSpec((1, nblk, vw, t), lambda bi, hg, i: (bi, 0, hg, 0)),
            _const_spec(later_mat.shape),
        ],
        out_specs=pl.BlockSpec((1, t, vw), lambda bi, hg, i: (bi, i, hg)),
        out_shape=jax.ShapeDtypeStruct((b, s, width), BF16),
        scratch_shapes=_attn_scratch(t)[1:],
        compiler_params=_params("parallel", "parallel", "arbitrary"),
        name="sb_attention",
    )(sq, sk, svt, later_mat)


def _post_kernel(x_ref, yf_ref, ys_ref, gf_ref, gs_ref, p_ref,
                 wbf_ref, wbs_ref, wo_ref, g2_ref, wg_ref, wu_ref, wd_ref, gp_ref, wpg_ref, wpp_ref,
                 o_ref):
    merged = (gf_ref[...].astype(F32) * _dot(yf_ref[...], wbf_ref[...])
              + gs_ref[...].astype(F32) * _dot(ys_ref[...], wbs_ref[...]))
    x = x_ref[...] + _dot(merged.astype(BF16), wo_ref[...])

    h = _rms_rows(x, g2_ref[...]).astype(BF16)
    a = _dot(h, wg_ref[...])
    u = _dot(h, wu_ref[...])
    x = x + 0.5 * _dot((a * _sigmoid(a) * u).astype(BF16), wd_ref[...])

    h = _rms_rows(x, gp_ref[...]).astype(BF16)
    gate = _sigmoid(_dot(h, wpg_ref[...]))
    o_ref[...] = x + gate * _dot(p_ref[...].astype(BF16), wpp_ref[...])


def _post(x2d, yf, ys, gf, gs, p2d, weights):
    n, d = x2d.shape
    tm = POST_TILE

    def tok(a):
        return pl.BlockSpec((tm, a.shape[1]), lambda i: (i, 0))

    toks = [x2d, yf, ys, gf, gs, p2d]
    return pl.pallas_call(
        _post_kernel,
        grid=(n // tm,),
        in_specs=[tok(a) for a in toks] + [_const_spec(w.shape) for w in weights],
        out_specs=tok(x2d),
        out_shape=jax.ShapeDtypeStruct((n, d), F32),
        compiler_params=_params("parallel"),
        name="post",
    )(*toks, *weights)


@functools.lru_cache(maxsize=None)
def _layout_constants(n_heads, tile):
    width = n_heads * HEAD_DIM
    sel = np.zeros((width, LANES), np.float32)
    pq = np.zeros((3 * LANES, width), np.float32)
    pk = np.zeros((3 * LANES, width), np.float32)
    cq = np.zeros((1, width), np.float32)
    ck = np.zeros((1, width), np.float32)
    for h in range(n_heads):
        sel[h * HEAD_DIM:(h + 1) * HEAD_DIM, h] = 1.0 / HEAD_DIM
        base = (h // 2) * LANES + (h % 2) * N_FEATURES
        for part in range(3):
            pq[part * LANES + h, base + part] = 1.0
            pk[part * LANES + h, base + 3 + part] = -1.0
            cq[0, base + 3 + part] = 1.0
            ck[0, base + part] = 1.0
    selt = (sel.T > 0).astype(np.float32)
    tri = np.tril(np.ones((tile, tile), np.float32))
    return sel, selt, tri, pq, pk, cq, ck


@functools.lru_cache(maxsize=None)
def _later_matrix(tile):
    return np.concatenate([np.triu(np.ones((tile, tile), np.float32), k=1),
                           np.ones((ONES_ROWS, tile), np.float32)], axis=0)


def kernel(x, p, ffn1_norm, ffn1_w_gate, ffn1_w_up, ffn1_w_down, mix_norm, w_in, forget_bias, q_norm, k_norm, w_branch_fox, w_branch_sb, w_out, ffn2_norm, ffn2_w_gate, ffn2_w_up, ffn2_w_down, ple_norm, w_ple_gate, w_ple_proj):
    b, s, d = x.shape
    depth = w_in.shape[0]
    fox_w = w_branch_fox.shape[1]
    sb_w = w_branch_sb.shape[1]
    n_heads = forget_bias.shape[1]
    assert fox_w == n_heads * HEAD_DIM and sb_w == fox_w and n_heads <= LANES
    assert s % TOKEN_TILE == 0 and TOKEN_TILE % ATTN_TILE == 0 and TOKEN_TILE // ATTN_TILE <= SUBLANES
    assert n_heads % HEADS_PER_STEP == 0 and 2 * N_FEATURES <= LANES

    sel, selt, tri, pq, pk, cq, ck = _layout_constants(n_heads, TOKEN_TILE)
    later = jnp.asarray(_later_matrix(ATTN_TILE), BF16)

    xf = x.reshape(b * s, d)
    for i in range(depth):
        xf = _ffn(xf, ffn1_norm[i][None], ffn1_w_gate[i].astype(BF16),
                  ffn1_w_up[i].astype(BF16), ffn1_w_down[i].astype(BF16))

        w = w_in[i]
        o = 0
        wq = w[:, o:o + fox_w]; o += fox_w
        wk = w[:, o:o + fox_w]; o += fox_w
        wvt = w[:, o:o + fox_w].T; o += fox_w
        wf = jnp.pad(w[:, o:o + n_heads], ((0, 0), (0, LANES - n_heads))); o += n_heads
        wsqk = w[:, o:o + 2 * sb_w]; o += 2 * sb_w
        wsvt = w[:, o:o + sb_w].T; o += sb_w
        wgate = w[:, o:o + 2 * d]
        fbias = jnp.pad(forget_bias[i][None], ((0, 0), (0, LANES - n_heads)))
        gq = jnp.tile(q_norm[i], n_heads)[None] * (LOG2E * HEAD_DIM ** -0.5)
        gk = jnp.tile(k_norm[i], n_heads)[None]
        consts = [mix_norm[i][None], wq.astype(BF16), wk.astype(BF16), wvt.astype(BF16),
                  wsqk.astype(BF16), wsvt.astype(BF16), wgate.astype(BF16), wf.astype(BF16),
                  fbias, gq, gk,
                  jnp.asarray(sel, BF16), jnp.asarray(selt, BF16), jnp.asarray(tri, BF16),
                  jnp.asarray(pq, BF16), jnp.asarray(pk, BF16), jnp.asarray(cq), jnp.asarray(ck)]
        qf, kf, fvt, sq, sk, svt, gf, gs, fend = _inproj(xf.reshape(b, s, d), consts)

        fend = fend[:, :, :TOKEN_TILE // ATTN_TILE, :n_heads].reshape(b, s // ATTN_TILE, n_heads)
        fend = fend.transpose(0, 2, 1).reshape(b * n_heads, s // ATTN_TILE)
        ub = LOG2E * HEAD_DIM ** 0.5 * jnp.max(jnp.abs(q_norm[i])) * jnp.max(jnp.abs(k_norm[i]))
        margin = (2.0 * ub + ZERO_WEIGHT_BITS).reshape(1).astype(F32)

        y_fox = _fox(fend, margin, qf, kf, fvt)
        y_sb = _sb(sq, sk, svt, later)

        weights = [w_branch_fox[i].astype(BF16), w_branch_sb[i].astype(BF16), w_out[i].astype(BF16),
                   ffn2_norm[i][None], ffn2_w_gate[i].astype(BF16), ffn2_w_up[i].astype(BF16),
                   ffn2_w_down[i].astype(BF16), ple_norm[i][None],
                   w_ple_gate[i].astype(BF16), w_ple_proj[i].astype(BF16)]
        xf = _post(xf, y_fox.reshape(b * s, fox_w), y_sb.reshape(b * s, sb_w),
                   gf.reshape(b * s, d), gs.reshape(b * s, d), p[i].reshape(b * s, -1), weights)
    return xf.reshape(b, s, d)
```

```python
import functools

import jax
import jax.numpy as jnp
import numpy as np
from jax import lax
from jax.experimental import pallas as pl
from jax.experimental.pallas import tpu as pltpu

F32 = jnp.float32
BF16 = jnp.bfloat16

EPS = 1e-6
HEAD_DIM = 64
LOG2E = 1.4426950408889634
LANES = 128
SUBLANES = 8
ZERO_WEIGHT_BITS = 152.0
VMEM_LIMIT_BYTES = 56 * 1024 * 1024

TOKEN_TILE = 512
POST_TILE = 512
ATTN_TILE = 256

NT_DIMS = (((1,), (1,)), ((), ()))


def _dot(a, b):
    return jnp.dot(a, b, preferred_element_type=F32)


def _dot_nt(a, b):
    return lax.dot_general(a, b, NT_DIMS, preferred_element_type=F32)


def _split2(a):
    hi = a.astype(BF16)
    lo = (a - hi.astype(F32)).astype(BF16)
    return hi, lo


def _split3(a):
    p1 = a.astype(BF16)
    r1 = a - p1.astype(F32)
    p2 = r1.astype(BF16)
    r2 = r1 - p2.astype(F32)
    return p1, p2, r2.astype(BF16)


def _rms_rows(x, g):
    ms = jnp.mean(x * x, axis=-1, keepdims=True)
    return x * lax.rsqrt(ms + EPS) * g


def _sigmoid(x):
    return 1.0 / (1.0 + jnp.exp(-x))


def _params(*sem):
    return pltpu.CompilerParams(dimension_semantics=sem, vmem_limit_bytes=VMEM_LIMIT_BYTES)


def _const_spec(shape):
    nd = len(shape)
    return pl.BlockSpec(shape, lambda *_: (0,) * nd, pipeline_mode=pl.Buffered(1))


def _ffn_kernel(x_ref, g_ref, wg_ref, wu_ref, wd_ref, o_ref):
    x = x_ref[...]
    h = _rms_rows(x, g_ref[...]).astype(BF16)
    a = _dot(h, wg_ref[...])
    u = _dot(h, wu_ref[...])
    act = (a * _sigmoid(a) * u).astype(BF16)
    o_ref[...] = x + 0.5 * _dot(act, wd_ref[...])


def _ffn(x2d, g, wg, wu, wd):
    n, d = x2d.shape
    tm = TOKEN_TILE
    return pl.pallas_call(
        _ffn_kernel,
        grid=(n // tm,),
        in_specs=[pl.BlockSpec((tm, d), lambda i: (i, 0)), _const_spec(g.shape),
                  _const_spec(wg.shape), _const_spec(wu.shape), _const_spec(wd.shape)],
        out_specs=pl.BlockSpec((tm, d), lambda i: (i, 0)),
        out_shape=jax.ShapeDtypeStruct((n, d), F32),
        compiler_params=_params("parallel"),
        name="ffn",
    )(x2d, g, wg, wu, wd)


def _inproj_kernel(x_ref, g_ref, wq_ref, wk_ref, wvt_ref, wsqk_ref, wsvt_ref, wgate_ref, wf_ref,
                   fbias_ref, gq_ref, gk_ref, sel_ref, selt_ref, tri_ref,
                   pq_ref, pk_ref, cq_ref, ck_ref,
                   qf_ref, kf_ref, fvt_ref, sq_ref, sk_ref, svt_ref, gf_ref, gs_ref, fend_ref,
                   carry_scr):
    @pl.when(pl.program_id(1) == 0)
    def _():
        carry_scr[...] = jnp.zeros_like(carry_scr)

    h = _rms_rows(x_ref[0], g_ref[...]).astype(BF16)

    yf = _dot(h, wf_ref[...]) + fbias_ref[...]
    lf = jnp.minimum(yf, 0.0) - jnp.log1p(jnp.exp(-jnp.abs(yf)))
    lane = lax.broadcasted_iota(jnp.int32, lf.shape, 1)
    n_heads = qf_ref.shape[2] // LANES
    lf = jnp.where(lane < n_heads, lf, 0.0)
    c = _dot(tri_ref[...], jnp.concatenate(_split3(lf), axis=1))
    f_cum = c[:, :LANES] + c[:, LANES:2 * LANES] + c[:, 2 * LANES:] + carry_scr[...]
    carry_scr[...] = f_cum[-1:, :]
    f_bits = f_cum * LOG2E
    fparts = jnp.concatenate(_split3(f_bits), axis=1)
    t = fvt_ref.shape[3]
    fend_ref[0, 0] = jnp.zeros(fend_ref.shape[2:], F32)
    for c in range(fvt_ref.shape[1]):
        fend_ref[0, 0, c:c + 1, :] = f_bits[(c + 1) * t - 1:(c + 1) * t, :]

    def headnorm(y, gain):
        ms = _dot((y * y).astype(BF16), sel_ref[...])
        rhi, rlo = _split2(lax.rsqrt(ms + EPS))
        return y * (_dot(rhi, selt_ref[...]) + _dot(rlo, selt_ref[...])) * gain

    q = headnorm(_dot(h, wq_ref[...]), gq_ref[...]).astype(BF16)
    k = headnorm(_dot(h, wk_ref[...]), gk_ref[...]).astype(BF16)
    featq = (_dot(fparts, pq_ref[...]) + cq_ref[...]).astype(BF16)
    featk = (_dot(fparts, pk_ref[...]) + ck_ref[...]).astype(BF16)
    for pr in range(q.shape[1] // LANES):
        src = slice(pr * LANES, (pr + 1) * LANES)
        qf_ref[0, :, 2 * pr * LANES:(2 * pr + 1) * LANES] = q[:, src]
        qf_ref[0, :, (2 * pr + 1) * LANES:(2 * pr + 2) * LANES] = featq[:, src]
        kf_ref[0, :, 2 * pr * LANES:(2 * pr + 1) * LANES] = k[:, src]
        kf_ref[0, :, (2 * pr + 1) * LANES:(2 * pr + 2) * LANES] = featk[:, src]

    fvt =_dot_nt(wvt_ref[...], h).astype(BF16)
    svt = _dot_nt(wsvt_ref[...], h).astype(BF16)
    for c in range(fvt_ref.shape[1]):
        fvt_ref[0, c] = fvt[:, c * t:(c + 1) * t]
        svt_ref[0, c] = svt[:, c * t:(c + 1) * t]

    width = sq_ref.shape[2]
    ysb = _dot(h, wsqk_ref[...])
    sq_ref[0] = (ysb[:, :width] * (LOG2E * HEAD_DIM ** -0.5)).astype(BF16)
    sk_ref[0] = ysb[:, width:].astype(BF16)

    d = gf_ref.shape[2]
    yg = _sigmoid(_dot(h, wgate_ref[...]))
    gf_ref[0] = yg[:, :d].astype(BF16)
    gs_ref[0] = yg[:, d:].astype(BF16)


def _inproj(x3d, consts):
    b, s, d = x3d.shape
    tm = TOKEN_TILE
    t = ATTN_TILE
    width = consts[3].shape[0]
    spread = 2 * width
    ins = [x3d] + list(consts)
    in_specs = [pl.BlockSpec((1, tm, d), lambda bi, i: (bi, i, 0))]
    in_specs += [_const_spec(c.shape) for c in consts]

    def tok(n):
        return pl.BlockSpec((1, tm, n), lambda bi, i: (bi, i, 0))

    def shp(n):
        return jax.ShapeDtypeStruct((b, s, n), BF16)

    vt_spec = pl.BlockSpec((1, tm // t, width, t), lambda bi, i: (bi, i, 0, 0))
    vt_shape = jax.ShapeDtypeStruct((b, s // t, width, t), BF16)
    fend_spec = pl.BlockSpec((1, 1, SUBLANES, LANES), lambda bi, i: (bi, i, 0, 0))
    fend_shape = jax.ShapeDtypeStruct((b, s // tm, SUBLANES, LANES), F32)

    return pl.pallas_call(
        _inproj_kernel,
        grid=(b, s // tm),
        in_specs=in_specs,
        out_specs=[tok(spread), tok(spread), vt_spec, tok(width), tok(width), vt_spec,
                   tok(d), tok(d), fend_spec],
        out_shape=[shp(spread), shp(spread), vt_shape, shp(width), shp(width), vt_shape,
                   shp(d), shp(d), fend_shape],
        scratch_shapes=[pltpu.VMEM((1, LANES), F32)],
        compiler_params=_params("parallel", "arbitrary"),
        name="inproj",
    )(*ins)


HEADS_PER_STEP = 8
PAIRS_PER_STEP = HEADS_PER_STEP // 2
N_FEATURES = 6


def _key_query_iota(t):
    return (lax.broadcasted_iota(jnp.int32, (t, t), 0), lax.broadcasted_iota(jnp.int32, (t, t), 1))


def _store_heads(o_ref, outs):
    for pr in range(PAIRS_PER_STEP):
        pair = jnp.concatenate([outs[2 * pr], outs[2 * pr + 1]], axis=0)
        o_ref[0, :, pr * LANES:(pr + 1) * LANES] = pair.T.astype(o_ref.dtype)


def _value_rows(vt_ref, j, h):
    return vt_ref[0, j, h * HEAD_DIM:(h + 1) * HEAD_DIM, :]


def _fox_kernel(fend_ref, margin_ref, qf_ref, kf_ref, vt_ref, o_ref, m_scr, l_scr, acc_scr, s_scr):
    bi = pl.program_id(0)
    hg = pl.program_id(1)
    i = pl.program_id(2)
    t = qf_ref.shape[1]
    heads = range(HEADS_PER_STEP)

    def tile_needed(j):
        need = False
        for h in heads:
            row = (bi * pl.num_programs(1) + hg) * HEADS_PER_STEP + h
            gap = fend_ref[row, j] - fend_ref[row, jnp.maximum(i - 1, 0)]
            need = jnp.logical_or(need, gap <= margin_ref[0])
        return need

    first = lax.while_loop(
        lambda j: jnp.logical_and(j > 0, tile_needed(jnp.maximum(j - 1, 0))), lambda j: j - 1, i)

    key, qry = _key_query_iota(t)
    causal = key <= qry
    lane = lax.broadcasted_iota(jnp.int32, (t, 2 * LANES), 1)
    q = []
    for h in heads:
        qpair = qf_ref[0, :, (h // 2) * 2 * LANES:(h // 2 + 1) * 2 * LANES]
        q_lo, f_lo = (h % 2) * HEAD_DIM, LANES + (h % 2) * N_FEATURES
        own = jnp.logical_or(jnp.logical_and(lane >= q_lo, lane < q_lo + HEAD_DIM),
                             jnp.logical_and(lane >= f_lo, lane < f_lo + N_FEATURES))
        q.append(jnp.where(own, qpair, jnp.zeros_like(qpair)))
    m_scr[...] = jnp.full_like(m_scr, -jnp.inf)
    l_scr[...] = jnp.zeros_like(l_scr)
    acc_scr[...] = jnp.zeros_like(acc_scr)

    ones_rows = jnp.ones((ONES_ROWS, t), BF16)

    def logits(j):
        start = pl.multiple_of(j * t, t)
        return [_dot_nt(kf_ref[0, pl.ds(start, t), (h // 2) * 2 * LANES:(h // 2 + 1) * 2 * LANES], q[h])
                for h in heads]

    def softmax_step(s, j, diagonal):
        if diagonal:
            s = [jnp.where(causal, sh, -jnp.inf) for sh in s]
        m_old = [m_scr[h] for h in heads]
        m_new = [jnp.maximum(m_old[h], jnp.max(s[h], axis=0, keepdims=True)) for h in heads]
        alpha = [jnp.exp2(m_old[h] - m_new[h]) for h in heads]
        p = [jnp.exp2(s[h] - m_new[h]).astype(BF16) for h in heads]
        pv = [_dot(jnp.concatenate([_value_rows(vt_ref, j, h), ones_rows], axis=0), p[h])
              for h in heads]
        for h in heads:
            m_scr[h] = m_new[h]
            l_scr[h] = alpha[h] * l_scr[h] + pv[h][HEAD_DIM:HEAD_DIM + 1, :]
            acc_scr[h] = alpha[h] * acc_scr[h] + pv[h][:HEAD_DIM, :]

    def visit(j, carry):
        s_next = logits(j + 1)
        softmax_step([s_scr[h] for h in heads], j, False)
        for h in heads:
            s_scr[h] = s_next[h]
        return carry

    s_first = logits(first)
    for h in heads:
        s_scr[h] = s_first[h]
    lax.fori_loop(first, i, visit, 0)
    softmax_step([s_scr[h] for h in heads], i, True)
    _store_heads(o_ref, [acc_scr[h] * (1.0 / l_scr[h]) for h in heads])


ONES_ROWS = 16


def _attn_scratch(t):
    n = HEADS_PER_STEP
    return [pltpu.VMEM((n, 1, t), F32), pltpu.VMEM((n, 1, t), F32), pltpu.VMEM((n, HEAD_DIM, t), F32),
            pltpu.VMEM((n, t, t), F32)]


def _fox(fend, margin, qf, kf, fvt):
    b, nblk, width, t = fvt.shape
    s = nblk * t
    vw = PAIRS_PER_STEP * LANES
    qw = HEADS_PER_STEP * LANES
    smem = pl.BlockSpec(memory_space=pltpu.SMEM)
    return pl.pallas_call(
        _fox_kernel,
        grid=(b, width // vw, nblk),
        in_specs=[
            smem, smem,
            pl.BlockSpec((1, t, qw), lambda bi, hg, i: (bi, i, hg)),
            pl.BlockSpec((1, s, qw), lambda bi, hg, i: (bi, 0, hg)),
            pl.BlockSpec((1, nblk, vw, t), lambda bi, hg, i: (bi, 0, hg, 0)),
        ],
        out_specs=pl.BlockSpec((1, t, vw), lambda bi, hg, i: (bi, i, hg)),
        out_shape=jax.ShapeDtypeStruct((b, s, width), BF16),
        scratch_shapes=_attn_scratch(t),
        compiler_params=_params("parallel", "parallel", "arbitrary"),
        name="fox_attention",
    )(fend, margin, qf, kf, fvt)


def _sb_kernel(q_ref, k_ref, vt_ref, u_ref, o_ref, r_scr, acc_scr):
    i = pl.program_id(2)
    t = q_ref.shape[1]
    heads = range(HEADS_PER_STEP)
    key, qry = _key_query_iota(t)
    strict = key < qry
    qlane = lax.broadcasted_iota(jnp.int32, (t, LANES), 1)
    q = []
    for h in heads:
        qpair = q_ref[0, :, (h // 2) * LANES:(h // 2 + 1) * LANES]
        q.append(jnp.where((qlane < HEAD_DIM) == (h % 2 == 0), qpair, jnp.zeros_like(qpair)))
    r_scr[...] = jnp.zeros_like(r_scr)
    acc_scr[...] = jnp.zeros_like(acc_scr)

    def logits(j):
        start = pl.multiple_of(j * t, t)
        return [_dot_nt(k_ref[0, pl.ds(start, t), (h // 2) * LANES:(h // 2 + 1) * LANES], q[h])
                for h in heads]

    def stick_step(z, j, diagonal):
        sp = [jnp.maximum(zh, 0.0) + jnp.log2(1.0 + jnp.exp2(-jnp.abs(zh))) for zh in z]
        log_beta = [z[h] - sp[h] for h in heads]
        if diagonal:
            sp = [jnp.where(strict, sh, 0.0) for sh in sp]
        sums = [_dot(u_ref[...], sh.astype(BF16)) for sh in sp]
        a = [jnp.exp2(log_beta[h] - sums[h][:t, :]) for h in heads]
        if diagonal:
            a = [jnp.where(strict, ah, 0.0) for ah in a]
        pv = [_dot(_value_rows(vt_ref, j, h), a[h].astype(BF16)) for h in heads]
        for h in heads:
            r = r_scr[h]
            acc_scr[h] += jnp.exp2(-r) * pv[h]
            r_scr[h] = r + sums[h][t:t + 1, :]

    def least_r():
        r = r_scr[0]
        for h in heads[1:]:
            r = jnp.minimum(r, r_scr[h])
        return jnp.min(r)

    def body(carry):
        n, _ = carry
        stick_step(logits(i - 1 - n), i - 1 - n, False)
        return n + 1, least_r()

    stick_step(logits(i), i, True)
    lax.while_loop(lambda c: jnp.logical_and(c[0] < i, c[1] < ZERO_WEIGHT_BITS), body,
                   (jnp.int32(0), least_r()))
    _store_heads(o_ref, [acc_scr[h] for h in heads])


def _sb(sq, sk, svt, later_mat):
    b, nblk, width, t = svt.shape
    s = nblk * t
    vw = PAIRS_PER_STEP * LANES
    return pl.pallas_call(
        _sb_kernel,
        grid=(b, width // vw, nblk),
        in_specs=[
            pl.BlockSpec((1, t, vw), lambda bi, hg, i: (bi, i, hg)),
            pl.BlockSpec((1, s, vw), lambda bi, hg, i: (bi, 0, hg)),
            pl.BlockSpec((1, nblk, vw, t), lambda bi, hg, i: (bi, 0, hg, 0)),
            _const_spec(later_mat.shape),
        ],
        out_specs=pl.BlockSpec((1, t, vw), lambda bi, hg, i: (bi, i, hg)),
        out_shape=jax.ShapeDtypeStruct((b, s, width), BF16),
        scratch_shapes=_attn_scratch(t)[1:3],
        compiler_params=_params("parallel", "parallel", "arbitrary"),
        name="sb_attention",
    )(sq, sk, svt, later_mat)


def _post_kernel(x_ref, yf_ref, ys_ref, gf_ref, gs_ref, p_ref,
                 wbf_ref, wbs_ref, wo_ref, g2_ref, wg_ref, wu_ref, wd_ref, gp_ref, wpg_ref, wpp_ref,
                 o_ref):
    merged = (gf_ref[...].astype(F32) * _dot(yf_ref[...], wbf_ref[...])
              + gs_ref[...].astype(F32) * _dot(ys_ref[...], wbs_ref[...]))
    x = x_ref[...] + _dot(merged.astype(BF16), wo_ref[...])

    h = _rms_rows(x, g2_ref[...]).astype(BF16)
    a = _dot(h, wg_ref[...])
    u = _dot(h, wu_ref[...])
    x = x + 0.5 * _dot((a * _sigmoid(a) * u).astype(BF16), wd_ref[...])

    h = _rms_rows(x, gp_ref[...]).astype(BF16)
    gate = _sigmoid(_dot(h, wpg_ref[...]))
    o_ref[...] = x + gate * _dot(p_ref[...].astype(BF16), wpp_ref[...])


def _post(x2d, yf, ys, gf, gs, p2d, weights):
    n, d = x2d.shape
    tm = POST_TILE

    def tok(a):
        return pl.BlockSpec((tm, a.shape[1]), lambda i: (i, 0))

    toks = [x2d, yf, ys, gf, gs, p2d]
    return pl.pallas_call(
        _post_kernel,
        grid=(n // tm,),
        in_specs=[tok(a) for a in toks] + [_const_spec(w.shape) for w in weights],
        out_specs=tok(x2d),
        out_shape=jax.ShapeDtypeStruct((n, d), F32),
        compiler_params=_params("parallel"),
        name="post",
    )(*toks, *weights)


@functools.lru_cache(maxsize=None)
def _layout_constants(n_heads, tile):
    width = n_heads * HEAD_DIM
    sel = np.zeros((width, LANES), np.float32)
    pq = np.zeros((3 * LANES, width), np.float32)
    pk = np.zeros((3 * LANES, width), np.float32)
    cq = np.zeros((1, width), np.float32)
    ck = np.zeros((1, width), np.float32)
    for h in range(n_heads):
        sel[h * HEAD_DIM:(h + 1) * HEAD_DIM, h] = 1.0 / HEAD_DIM
        base = (h // 2) * LANES + (h % 2) * N_FEATURES
        for part in range(3):
            pq[part * LANES + h, base + part] = 1.0
            pk[part * LANES + h, base + 3 + part] = -1.0
            cq[0, base + 3 + part] = 1.0
            ck[0, base + part] = 1.0
    selt = (sel.T > 0).astype(np.float32)
    tri = np.tril(np.ones((tile, tile), np.float32))
    return sel, selt, tri, pq, pk, cq, ck


@functools.lru_cache(maxsize=None)
def _later_matrix(tile):
    return np.concatenate([np.triu(np.ones((tile, tile), np.float32), k=1),
                           np.ones((ONES_ROWS, tile), np.float32)], axis=0)


def kernel(x, p, ffn1_norm, ffn1_w_gate, ffn1_w_up, ffn1_w_down, mix_norm, w_in, forget_bias, q_norm, k_norm, w_branch_fox, w_branch_sb, w_out, ffn2_norm, ffn2_w_gate, ffn2_w_up, ffn2_w_down, ple_norm, w_ple_gate, w_ple_proj):
    b, s, d = x.shape
    depth = w_in.shape[0]
    fox_w = w_branch_fox.shape[1]
    sb_w = w_branch_sb.shape[1]
    n_heads = forget_bias.shape[1]
    assert fox_w == n_heads * HEAD_DIM and sb_w == fox_w and n_heads <= LANES
    assert s % TOKEN_TILE == 0 and TOKEN_TILE % ATTN_TILE == 0 and TOKEN_TILE // ATTN_TILE <= SUBLANES
    assert n_heads % HEADS_PER_STEP == 0 and 2 * N_FEATURES <= LANES

    sel, selt, tri, pq, pk, cq, ck = _layout_constants(n_heads, TOKEN_TILE)
    later = jnp.asarray(_later_matrix(ATTN_TILE), BF16)

    xf = x.reshape(b * s, d)
    for i in range(depth):
        xf = _ffn(xf, ffn1_norm[i][None], ffn1_w_gate[i].astype(BF16),
                  ffn1_w_up[i].astype(BF16), ffn1_w_down[i].astype(BF16))

        w = w_in[i]
        o = 0
        wq = w[:, o:o + fox_w]; o += fox_w
        wk = w[:, o:o + fox_w]; o += fox_w
        wvt = w[:, o:o + fox_w].T; o += fox_w
        wf = jnp.pad(w[:, o:o + n_heads], ((0, 0), (0, LANES - n_heads))); o += n_heads
        wsqk = w[:, o:o + 2 * sb_w]; o += 2 * sb_w
        wsvt = w[:, o:o + sb_w].T; o += sb_w
        wgate = w[:, o:o + 2 * d]
        fbias = jnp.pad(forget_bias[i][None], ((0, 0), (0, LANES - n_heads)))
        gq = jnp.tile(q_norm[i], n_heads)[None] * (LOG2E * HEAD_DIM ** -0.5)
        gk = jnp.tile(k_norm[i], n_heads)[None]
        consts = [mix_norm[i][None], wq.astype(BF16), wk.astype(BF16), wvt.astype(BF16),
                  wsqk.astype(BF16), wsvt.astype(BF16), wgate.astype(BF16), wf.astype(BF16),
                  fbias, gq, gk,
                  jnp.asarray(sel, BF16), jnp.asarray(selt, BF16), jnp.asarray(tri, BF16),
                  jnp.asarray(pq, BF16), jnp.asarray(pk, BF16), jnp.asarray(cq), jnp.asarray(ck)]
        qf, kf, fvt, sq, sk, svt, gf, gs, fend = _inproj(xf.reshape(b, s, d), consts)

        fend = fend[:, :, :TOKEN_TILE // ATTN_TILE, :n_heads].reshape(b, s // ATTN_TILE, n_heads)
        fend = fend.transpose(0, 2, 1).reshape(b * n_heads, s // ATTN_TILE)
        ub = LOG2E * HEAD_DIM ** 0.5 * jnp.max(jnp.abs(q_norm[i])) * jnp.max(jnp.abs(k_norm[i]))
        margin = (2.0 * ub + ZERO_WEIGHT_BITS).reshape(1).astype(F32)

        y_fox = _fox(fend, margin, qf, kf, fvt)
        y_sb = _sb(sq, sk, svt, later)

        weights = [w_branch_fox[i].astype(BF16), w_branch_sb[i].astype(BF16), w_out[i].astype(BF16),
                   ffn2_norm[i][None], ffn2_w_gate[i].astype(BF16), ffn2_w_up[i].astype(BF16),
                   ffn2_w_down[i].astype(BF16), ple_norm[i][None],
                   w_ple_gate[i].astype(BF16), w_ple_proj[i].astype(BF16)]
        xf = _post(xf, y_fox.reshape(b * s, fox_w), y_sb.reshape(b * s, sb_w),
                   gf.reshape(b * s, d), gs.reshape(b * s, d), p[i].reshape(b * s, -1), weights)
    return xf.reshape(b, s, d)
```

```python
import functools

import jax
import jax.numpy as jnp
import numpy as np
from jax import lax
from jax.experimental import pallas as pl
from jax.experimental.pallas import tpu as pltpu

F32 = jnp.float32
BF16 = jnp.bfloat16

EPS = 1e-6
HEAD_DIM = 64
LOG2E = 1.4426950408889634
LANES = 128
SUBLANES = 8
BF16_ROWS = 2 * SUBLANES
ZERO_WEIGHT_BITS = 152.0
VMEM_LIMIT_BYTES = 56 * 1024 * 1024

TOKEN_TILE = 512
POST_TILE = 512
ATTN_TILE = 256

NT_DIMS = (((1,), (1,)), ((), ()))


def _dot(a, b):
    return jnp.dot(a, b, preferred_element_type=F32)


def _dot_nt(a, b):
    return lax.dot_general(a, b, NT_DIMS, preferred_element_type=F32)


def _split2(a):
    hi = a.astype(BF16)
    lo = (a - hi.astype(F32)).astype(BF16)
    return hi, lo


def _split3(a):
    p1 = a.astype(BF16)
    r1 = a - p1.astype(F32)
    p2 = r1.astype(BF16)
    r2 = r1 - p2.astype(F32)
    return p1, p2, r2.astype(BF16)


def _rms_rows(x, g):
    ms = jnp.mean(x * x, axis=-1, keepdims=True)
    return x * lax.rsqrt(ms + EPS) * g


def _sigmoid(x):
    return 1.0 / (1.0 + jnp.exp(-x))


def _params(*sem):
    return pltpu.CompilerParams(dimension_semantics=sem, vmem_limit_bytes=VMEM_LIMIT_BYTES)


def _const_spec(shape):
    nd = len(shape)
    return pl.BlockSpec(shape, lambda *_: (0,) * nd, pipeline_mode=pl.Buffered(1))


def _ffn_kernel(x_ref, g_ref, wg_ref, wu_ref, wd_ref, o_ref):
    x = x_ref[...]
    h = _rms_rows(x, g_ref[...]).astype(BF16)
    a = _dot(h, wg_ref[...])
    u = _dot(h, wu_ref[...])
    act = (a * _sigmoid(a) * u).astype(BF16)
    o_ref[...] = x + 0.5 * _dot(act, wd_ref[...])


def _ffn(x2d, g, wg, wu, wd):
    n, d = x2d.shape
    tm = TOKEN_TILE
    return pl.pallas_call(
        _ffn_kernel,
        grid=(n // tm,),
        in_specs=[pl.BlockSpec((tm, d), lambda i: (i, 0)), _const_spec(g.shape),
                  _const_spec(wg.shape), _const_spec(wu.shape), _const_spec(wd.shape)],
        out_specs=pl.BlockSpec((tm, d), lambda i: (i, 0)),
        out_shape=jax.ShapeDtypeStruct((n, d), F32),
        compiler_params=_params("parallel"),
        name="ffn",
    )(x2d, g, wg, wu, wd)


def _inproj_kernel(x_ref, g_ref, wtok_ref, wvalt_ref,
                   fbias_ref, gq_ref, gk_ref, sel_ref, selt_ref, tri_ref,
                   pq_ref, pk_ref, cq_ref, ck_ref,
                   qf_ref, kf_ref, fvt_ref, sq_ref, sk_ref, svt_ref, gf_ref, gs_ref, fend_ref,
                   carry_scr):
    @pl.when(pl.program_id(1) == 0)
    def _():
        carry_scr[...] = jnp.zeros_like(carry_scr)

    h = _rms_rows(x_ref[0], g_ref[...]).astype(BF16)
    t = fvt_ref.shape[3]
    n_heads = qf_ref.shape[2] // LANES
    width = sq_ref.shape[2]
    d = gf_ref.shape[2]
    c_k, c_sb, c_gate, c_f = width, 2 * width, 4 * width, 4 * width + 2 * d

    yf = _dot(h, wtok_ref[:, c_f:]) + fbias_ref[...]
    yg = _dot(h, wtok_ref[:, c_gate:c_f])

    lf = jnp.minimum(yf, 0.0) - jnp.log1p(jnp.exp(-jnp.abs(yf)))
    lane = lax.broadcasted_iota(jnp.int32, lf.shape, 1)
    lf = jnp.where(lane < n_heads, lf, 0.0)
    c = _dot(tri_ref[...], jnp.concatenate(_split3(lf), axis=1))

    yq = _dot(h, wtok_ref[:, :c_k])
    yk = _dot(h, wtok_ref[:, c_k:c_sb])
    msq = _dot((yq * yq).astype(BF16), sel_ref[...])
    msk = _dot((yk * yk).astype(BF16), sel_ref[...])

    ysb = _dot(h, wtok_ref[:, c_sb:c_gate])

    def spread_rsqrt(ms):
        rhi, rlo = _split2(lax.rsqrt(ms + EPS))
        return _dot(rhi, selt_ref[...]) + _dot(rlo, selt_ref[...])

    rq = spread_rsqrt(msq)
    rk = spread_rsqrt(msk)

    fvt = _dot_nt(wvalt_ref[:width, :], h).astype(BF16)
    svt = _dot_nt(wvalt_ref[width:2 * width, :], h).astype(BF16)

    f_cum = c[:, :LANES] + c[:, LANES:2 * LANES] + c[:, 2 * LANES:] + carry_scr[...]
    carry_scr[...] = f_cum[-1:, :]
    f_bits = f_cum * LOG2E
    fparts = jnp.concatenate(_split3(f_bits), axis=1)
    featq = (_dot(fparts, pq_ref[...]) + cq_ref[...]).astype(BF16)
    featk = (_dot(fparts, pk_ref[...]) + ck_ref[...]).astype(BF16)

    yg = _sigmoid(yg)
    gf_ref[0] = yg[:, :d].astype(BF16)
    gs_ref[0] = yg[:, d:].astype(BF16)

    sq_ref[0] = (ysb[:, :width] * (LOG2E * HEAD_DIM ** -0.5)).astype(BF16)
    sk_ref[0] = ysb[:, width:].astype(BF16)

    for c in range(fvt_ref.shape[1]):
        fvt_ref[0, c] = fvt[:, c * t:(c + 1) * t]
        svt_ref[0, c] = svt[:, c * t:(c + 1) * t]

    fend_ref[0, 0] = jnp.zeros(fend_ref.shape[2:], F32)
    for c in range(fvt_ref.shape[1]):
        fend_ref[0, 0, c:c + 1, :] = f_bits[(c + 1) * t - 1:(c + 1) * t, :]

    q = (yq * rq * gq_ref[...]).astype(BF16)
    k = (yk * rk * gk_ref[...]).astype(BF16)
    for pr in range(q.shape[1] // LANES):
        src = slice(pr * LANES, (pr + 1) * LANES)
        qf_ref[0, :, 2 * pr * LANES:(2 * pr + 1) * LANES] = q[:, src]
        qf_ref[0, :, (2 * pr + 1) * LANES:(2 * pr + 2) * LANES] = featq[:, src]
        kf_ref[0, :, 2 * pr * LANES:(2 * pr + 1) * LANES] = k[:, src]
        kf_ref[0, :, (2 * pr + 1) * LANES:(2 * pr + 2) * LANES] = featk[:, src]


def _inproj(x3d, consts):
    b, s, d = x3d.shape
    tm = TOKEN_TILE
    t = ATTN_TILE
    width = consts[2].shape[0] // 2
    spread = 2 * width
    ins = [x3d] + list(consts)
    in_specs = [pl.BlockSpec((1, tm, d), lambda bi, i: (bi, i, 0))]
    in_specs += [_const_spec(c.shape) for c in consts]

    def tok(n):
        return pl.BlockSpec((1, tm, n), lambda bi, i: (bi, i, 0))

    def shp(n):
        return jax.ShapeDtypeStruct((b, s, n), BF16)

    vt_spec = pl.BlockSpec((1, tm // t, width, t), lambda bi, i: (bi, i, 0, 0))
    vt_shape = jax.ShapeDtypeStruct((b, s // t, width, t), BF16)
    fend_spec = pl.BlockSpec((1, 1, SUBLANES, LANES), lambda bi, i: (bi, i, 0, 0))
    fend_shape = jax.ShapeDtypeStruct((b, s // tm, SUBLANES, LANES), F32)

    return pl.pallas_call(
        _inproj_kernel,
        grid=(b, s // tm),
        in_specs=in_specs,
        out_specs=[tok(spread), tok(spread), vt_spec, tok(width), tok(width), vt_spec,
                   tok(d), tok(d), fend_spec],
        out_shape=[shp(spread), shp(spread), vt_shape, shp(width), shp(width), vt_shape,
                   shp(d), shp(d), fend_shape],
        scratch_shapes=[pltpu.VMEM((1, LANES), F32)],
        compiler_params=_params("parallel", "arbitrary"),
        name="inproj",
    )(*ins)


HEADS_PER_STEP = 8
PAIRS_PER_STEP = HEADS_PER_STEP // 2
N_FEATURES = 6


def _key_query_iota(t):
    return (lax.broadcasted_iota(jnp.int32, (t, t), 0), lax.broadcasted_iota(jnp.int32, (t, t), 1))


def _store_heads(o_ref, outs):
    for pr in range(PAIRS_PER_STEP):
        pair = jnp.concatenate([outs[2 * pr], outs[2 * pr + 1]], axis=0)
        o_ref[0, :, pr * LANES:(pr + 1) * LANES] = pair.T.astype(o_ref.dtype)


def _value_rows(vt_ref, j, h):
    return vt_ref[0, j, h * HEAD_DIM:(h + 1) * HEAD_DIM, :]


def _fox_kernel(fend_ref, margin_ref, qf_ref, kf_ref, vt_ref, o_ref, m_scr, l_scr, acc_scr, s_scr):
    bi = pl.program_id(0)
    hg = pl.program_id(1)
    i = pl.program_id(2)
    t = qf_ref.shape[1]
    heads = range(HEADS_PER_STEP)

    def tile_needed(j):
        need = False
        for h in heads:
            row = (bi * pl.num_programs(1) + hg) * HEADS_PER_STEP + h
            gap = fend_ref[row, j] - fend_ref[row, jnp.maximum(i - 1, 0)]
            need = jnp.logical_or(need, gap <= margin_ref[0])
        return need

    first = lax.while_loop(
        lambda j: jnp.logical_and(j > 0, tile_needed(jnp.maximum(j - 1, 0))), lambda j: j - 1, i)

    key, qry = _key_query_iota(t)
    causal = key <= qry
    lane = lax.broadcasted_iota(jnp.int32, (t, 2 * LANES), 1)
    q = []
    for h in heads:
        qpair = qf_ref[0, :, (h // 2) * 2 * LANES:(h // 2 + 1) * 2 * LANES]
        q_lo, f_lo = (h % 2) * HEAD_DIM, LANES + (h % 2) * N_FEATURES
        own = jnp.logical_or(jnp.logical_and(lane >= q_lo, lane < q_lo + HEAD_DIM),
                             jnp.logical_and(lane >= f_lo, lane < f_lo + N_FEATURES))
        q.append(jnp.where(own, qpair, jnp.zeros_like(qpair)))
    m_scr[...] = jnp.full_like(m_scr, -jnp.inf)
    l_scr[...] = jnp.zeros_like(l_scr)
    acc_scr[...] = jnp.zeros_like(acc_scr)

    ones_rows = jnp.ones((ONES_ROWS, t), BF16)

    def logits(j):
        start = pl.multiple_of(j * t, t)
        return [_dot_nt(kf_ref[0, pl.ds(start, t), (h // 2) * 2 * LANES:(h // 2 + 1) * 2 * LANES], q[h])
                for h in heads]

    def softmax_step(s, j, diagonal):
        if diagonal:
            s = [jnp.where(causal, sh, -jnp.inf) for sh in s]
        m_old = [m_scr[h] for h in heads]
        m_new = [jnp.maximum(m_old[h], jnp.max(s[h], axis=0, keepdims=True)) for h in heads]
        alpha = [jnp.exp2(m_old[h] - m_new[h]) for h in heads]
        p = [jnp.exp2(s[h] - m_new[h]).astype(BF16) for h in heads]
        pv = [_dot(jnp.concatenate([_value_rows(vt_ref, j, h), ones_rows], axis=0), p[h])
              for h in heads]
        for h in heads:
            m_scr[h] = m_new[h]
            l_scr[h] = alpha[h] * l_scr[h] + pv[h][HEAD_DIM:HEAD_DIM + 1, :]
            acc_scr[h] = alpha[h] * acc_scr[h] + pv[h][:HEAD_DIM, :]

    def visit(j, carry):
        s_next = logits(j + 1)
        softmax_step([s_scr[h] for h in heads], j, False)
        for h in heads:
            s_scr[h] = s_next[h]
        return carry

    s_first = logits(first)
    for h in heads:
        s_scr[h] = s_first[h]
    lax.fori_loop(first, i, visit, 0)
    softmax_step([s_scr[h] for h in heads], i, True)
    _store_heads(o_ref, [acc_scr[h] * (1.0 / l_scr[h]) for h in heads])


ONES_ROWS = BF16_ROWS


def _attn_scratch(t):
    n = HEADS_PER_STEP
    return [pltpu.VMEM((n, 1, t), F32), pltpu.VMEM((n, 1, t), F32), pltpu.VMEM((n, HEAD_DIM, t), F32),
            pltpu.VMEM((n, t, t), F32)]


def _fox(fend, margin, qf, kf, fvt):
    b, nblk, width, t = fvt.shape
    s = nblk * t
    vw = PAIRS_PER_STEP * LANES
    qw = HEADS_PER_STEP * LANES
    smem = pl.BlockSpec(memory_space=pltpu.SMEM)
    return pl.pallas_call(
        _fox_kernel,
        grid=(b, width // vw, nblk),
        in_specs=[
            smem, smem,
            pl.BlockSpec((1, t, qw), lambda bi, hg, i: (bi, i, hg)),
            pl.BlockSpec((1, s, qw), lambda bi, hg, i: (bi, 0, hg)),
            pl.BlockSpec((1, nblk, vw, t), lambda bi, hg, i: (bi, 0, hg, 0)),
        ],
        out_specs=pl.BlockSpec((1, t, vw), lambda bi, hg, i: (bi, i, hg)),
        out_shape=jax.ShapeDtypeStruct((b, s, width), BF16),
        scratch_shapes=_attn_scratch(t),
        compiler_params=_params("parallel", "parallel", "arbitrary"),
        name="fox_attention",
    )(fend, margin, qf, kf, fvt)


def _sb_kernel(q_ref, k_ref, vt_ref, u_ref, o_ref, r_scr, acc_scr):
    i = pl.program_id(2)
    t = q_ref.shape[1]
    heads = range(HEADS_PER_STEP)
    key, qry = _key_query_iota(t)
    strict = key < qry
    qlane = lax.broadcasted_iota(jnp.int32, (t, LANES), 1)
    q = []
    for h in heads:
        qpair = q_ref[0, :, (h // 2) * LANES:(h // 2 + 1) * LANES]
        q.append(jnp.where((qlane < HEAD_DIM) == (h % 2 == 0), qpair, jnp.zeros_like(qpair)))
    r_scr[...] = jnp.zeros_like(r_scr)
    acc_scr[...] = jnp.zeros_like(acc_scr)

    def logits(j):
        start = pl.multiple_of(j * t, t)
        return [_dot_nt(k_ref[0, pl.ds(start, t), (h // 2) * LANES:(h // 2 + 1) * LANES], q[h])
                for h in heads]

    def stick_step(z, j, diagonal):
        sp = [jnp.maximum(zh, 0.0) + jnp.log2(1.0 + jnp.exp2(-jnp.abs(zh))) for zh in z]
        log_beta = [z[h] - sp[h] for h in heads]
        if diagonal:
            sp = [jnp.where(strict, sh, 0.0) for sh in sp]
        sums = [_dot(u_ref[...], sh.astype(BF16)) for sh in sp]
        a = [jnp.exp2(log_beta[h] - sums[h][:t, :]) for h in heads]
        if diagonal:
            a = [jnp.where(strict, ah, 0.0) for ah in a]
        pv = [_dot(_value_rows(vt_ref, j, h), a[h].astype(BF16)) for h in heads]
        for h in heads:
            r = r_scr[h]
            acc_scr[h] += jnp.exp2(-r) * pv[h]
            r_scr[h] = r + sums[h][t:t + 1, :]

    def least_r():
        r = r_scr[0]
        for h in heads[1:]:
            r = jnp.minimum(r, r_scr[h])
        return jnp.min(r)

    def body(carry):
        n, _ = carry
        stick_step(logits(i - 1 - n), i - 1 - n, False)
        return n + 1, least_r()

    stick_step(logits(i), i, True)
    lax.while_loop(lambda c: jnp.logical_and(c[0] < i, c[1] < ZERO_WEIGHT_BITS), body,
                   (jnp.int32(0), least_r()))
    _store_heads(o_ref, [acc_scr[h] for h in heads])


def _sb(sq, sk, svt, later_mat):
    b, nblk, width, t = svt.shape
    s = nblk * t
    vw = PAIRS_PER_STEP * LANES
    return pl.pallas_call(
        _sb_kernel,
        grid=(b, width // vw, nblk),
        in_specs=[
            pl.BlockSpec((1, t, vw), lambda bi, hg, i: (bi, i, hg)),
            pl.BlockSpec((1, s, vw), lambda bi, hg, i: (bi, 0, hg)),
            pl.BlockSpec((1, nblk, vw, t), lambda bi, hg, i: (bi, 0, hg, 0)),
            _const_spec(later_mat.shape),
        ],
        out_specs=pl.BlockSpec((1, t, vw), lambda bi, hg, i: (bi, i, hg)),
        out_shape=jax.ShapeDtypeStruct((b, s, width), BF16),
        scratch_shapes=_attn_scratch(t)[1:3],
        compiler_params=_params("parallel", "parallel", "arbitrary"),
        name="sb_attention",
    )(sq, sk, svt, later_mat)


def _post_kernel(x_ref, yf_ref, ys_ref, gf_ref, gs_ref, p_ref,
                 wbf_ref, wbs_ref, wo_ref, g2_ref, wg_ref, wu_ref, wd_ref, gp_ref, wpg_ref, wpp_ref,
                 o_ref):
    merged = (gf_ref[...].astype(F32) * _dot(yf_ref[...], wbf_ref[...])
              + gs_ref[...].astype(F32) * _dot(ys_ref[...], wbs_ref[...]))
    x = x_ref[...] + _dot(merged.astype(BF16), wo_ref[...])

    h = _rms_rows(x, g2_ref[...]).astype(BF16)
    a = _dot(h, wg_ref[...])
    u = _dot(h, wu_ref[...])
    x = x + 0.5 * _dot((a * _sigmoid(a) * u).astype(BF16), wd_ref[...])

    h = _rms_rows(x, gp_ref[...]).astype(BF16)
    gate = _sigmoid(_dot(h, wpg_ref[...]))
    o_ref[...] = x + gate * _dot(p_ref[...].astype(BF16), wpp_ref[...])


def _post(x2d, yf, ys, gf, gs, p2d, weights):
    n, d = x2d.shape
    tm = POST_TILE

    def tok(a):
        return pl.BlockSpec((tm, a.shape[1]), lambda i: (i, 0))

    toks = [x2d, yf, ys, gf, gs, p2d]
    return pl.pallas_call(
        _post_kernel,
        grid=(n // tm,),
        in_specs=[tok(a) for a in toks] + [_const_spec(w.shape) for w in weights],
        out_specs=tok(x2d),
        out_shape=jax.ShapeDtypeStruct((n, d), F32),
        compiler_params=_params("parallel"),
        name="post",
    )(*toks, *weights)


@functools.lru_cache(maxsize=None)
def _layout_constants(n_heads, tile):
    width = n_heads * HEAD_DIM
    sel = np.zeros((width, LANES), np.float32)
    pq = np.zeros((3 * LANES, width), np.float32)
    pk = np.zeros((3 * LANES, width), np.float32)
    cq = np.zeros((1, width), np.float32)
    ck = np.zeros((1, width), np.float32)
    for h in range(n_heads):
        sel[h * HEAD_DIM:(h + 1) * HEAD_DIM, h] = 1.0 / HEAD_DIM
        base = (h // 2) * LANES + (h % 2) * N_FEATURES
        for part in range(3):
            pq[part * LANES + h, base + part] = 1.0
            pk[part * LANES + h, base + 3 + part] = -1.0
            cq[0, base + 3 + part] = 1.0
            ck[0, base + part] = 1.0
    selt = (sel.T > 0).astype(np.float32)
    tri = np.tril(np.ones((tile, tile), np.float32))
    return sel, selt, tri, pq, pk, cq, ck


@functools.lru_cache(maxsize=None)
def _later_matrix(tile):
    return np.concatenate([np.triu(np.ones((tile, tile), np.float32), k=1),
                           np.ones((ONES_ROWS, tile), np.float32)], axis=0)


def kernel(x, p, ffn1_norm, ffn1_w_gate, ffn1_w_up, ffn1_w_down, mix_norm, w_in, forget_bias, q_norm, k_norm, w_branch_fox, w_branch_sb, w_out, ffn2_norm, ffn2_w_gate, ffn2_w_up, ffn2_w_down, ple_norm, w_ple_gate, w_ple_proj):
    b, s, d = x.shape
    depth = w_in.shape[0]
    fox_w = w_branch_fox.shape[1]
    sb_w = w_branch_sb.shape[1]
    n_heads = forget_bias.shape[1]
    assert fox_w == n_heads * HEAD_DIM and sb_w == fox_w and n_heads <= LANES
    assert s % TOKEN_TILE == 0 and TOKEN_TILE % ATTN_TILE == 0 and TOKEN_TILE // ATTN_TILE <= SUBLANES
    assert n_heads % HEADS_PER_STEP == 0 and 2 * N_FEATURES <= LANES

    sel, selt, tri, pq, pk, cq, ck = _layout_constants(n_heads, TOKEN_TILE)
    later = jnp.asarray(_later_matrix(ATTN_TILE), BF16)

    xf = x.reshape(b * s, d)
    for i in range(depth):
        xf = _ffn(xf, ffn1_norm[i][None], ffn1_w_gate[i].astype(BF16),
                  ffn1_w_up[i].astype(BF16), ffn1_w_down[i].astype(BF16))

        w = w_in[i]
        o_f = 3 * fox_w
        o_sb = o_f + n_heads
        o_gate = o_sb + 3 * sb_w
        wf = jnp.pad(w[:, o_f:o_sb], ((0, 0), (0, LANES - n_heads)))
        w_tok = jnp.concatenate([w[:, :2 * fox_w], w[:, o_sb:o_sb + 2 * sb_w], w[:, o_gate:], wf],
                                axis=1).astype(BF16)
        w_val_t = jnp.concatenate([w[:, 2 * fox_w:o_f], w[:, o_sb + 2 * sb_w:o_gate]],
                                  axis=1).T.astype(BF16)
        fbias = jnp.pad(forget_bias[i][None], ((0, 0), (0, LANES - n_heads)))
        gq = jnp.tile(q_norm[i], n_heads)[None] * (LOG2E * HEAD_DIM ** -0.5)
        gk = jnp.tile(k_norm[i], n_heads)[None]
        consts = [mix_norm[i][None], w_tok, w_val_t, fbias, gq, gk,
                  jnp.asarray(sel, BF16), jnp.asarray(selt, BF16), jnp.asarray(tri, BF16),
                  jnp.asarray(pq, BF16), jnp.asarray(pk, BF16), jnp.asarray(cq), jnp.asarray(ck)]
        qf, kf, fvt, sq, sk, svt, gf, gs, fend = _inproj(xf.reshape(b, s, d), consts)

        fend = fend[:, :, :TOKEN_TILE // ATTN_TILE, :n_heads].reshape(b, s // ATTN_TILE, n_heads)
        fend = fend.transpose(0, 2, 1).reshape(b * n_heads, s // ATTN_TILE)
        ub = LOG2E * HEAD_DIM ** 0.5 * jnp.max(jnp.abs(q_norm[i])) * jnp.max(jnp.abs(k_norm[i]))
        margin = (2.0 * ub + ZERO_WEIGHT_BITS).reshape(1).astype(F32)

        y_fox = _fox(fend, margin, qf, kf, fvt)
        y_sb = _sb(sq, sk, svt, later)

        weights = [w_branch_fox[i].astype(BF16), w_branch_sb[i].astype(BF16), w_out[i].astype(BF16),
                   ffn2_norm[i][None], ffn2_w_gate[i].astype(BF16), ffn2_w_up[i].astype(BF16),
                   ffn2_w_down[i].astype(BF16), ple_norm[i][None],
                   w_ple_gate[i].astype(BF16), w_ple_proj[i].astype(BF16)]
        xf = _post(xf, y_fox.reshape(b * s, fox_w), y_sb.reshape(b * s, sb_w),
                   gf.reshape(b * s, d), gs.reshape(b * s, d), p[i].reshape(b * s, -1), weights)
    return xf.reshape(b, s, d)
```

```python
import functools

import jax
import jax.numpy as jnp
import numpy as np
from jax import lax
from jax.experimental import pallas as pl
from jax.experimental.pallas import tpu as pltpu

F32 = jnp.float32
BF16 = jnp.bfloat16

EPS = 1e-6
HEAD_DIM = 64
LOG2E = 1.4426950408889634
LANES = 128
SUBLANES = 8
BF16_ROWS = 2 * SUBLANES
ZERO_WEIGHT_BITS = 152.0
VMEM_LIMIT_BYTES = 56 * 1024 * 1024

TOKEN_TILE = 512
FFN_TILE = 1024
POST_TILE = 512
ATTN_TILE = 256

NT_DIMS = (((1,), (1,)), ((), ()))


def _dot(a, b):
    return jnp.dot(a, b, preferred_element_type=F32)


def _dot_nt(a, b):
    return lax.dot_general(a, b, NT_DIMS, preferred_element_type=F32)


def _split2(a):
    hi = a.astype(BF16)
    lo = (a - hi.astype(F32)).astype(BF16)
    return hi, lo


def _split3(a):
    p1 = a.astype(BF16)
    r1 = a - p1.astype(F32)
    p2 = r1.astype(BF16)
    r2 = r1 - p2.astype(F32)
    return p1, p2, r2.astype(BF16)


def _rms_rows(x, g):
    ms = jnp.mean(x * x, axis=-1, keepdims=True)
    return x * lax.rsqrt(ms + EPS) * g


def _sigmoid(x):
    return 1.0 / (1.0 + jnp.exp(-x))


def _params(*sem):
    return pltpu.CompilerParams(dimension_semantics=sem, vmem_limit_bytes=VMEM_LIMIT_BYTES)


def _const_spec(shape):
    nd = len(shape)
    return pl.BlockSpec(shape, lambda *_: (0,) * nd, pipeline_mode=pl.Buffered(1))


def _ffn_kernel(x_ref, g_ref, wg_ref, wu_ref, wd_ref, o_ref):
    x = x_ref[...]
    h = _rms_rows(x, g_ref[...]).astype(BF16)
    a = _dot(h, wg_ref[...])
    u = _dot(h, wu_ref[...])
    act = (a * _sigmoid(a) * u).astype(BF16)
    o_ref[...] = x + 0.5 * _dot(act, wd_ref[...])


def _ffn(x2d, g, wg, wu, wd):
    n, d = x2d.shape
    tm = FFN_TILE
    return pl.pallas_call(
        _ffn_kernel,
        grid=(n // tm,),
        in_specs=[pl.BlockSpec((tm, d), lambda i: (i, 0)), _const_spec(g.shape),
                  _const_spec(wg.shape), _const_spec(wu.shape), _const_spec(wd.shape)],
        out_specs=pl.BlockSpec((tm, d), lambda i: (i, 0)),
        out_shape=jax.ShapeDtypeStruct((n, d), F32),
        compiler_params=_params("parallel"),
        name="ffn",
    )(x2d, g, wg, wu, wd)


def _inproj_kernel(x_ref, g_ref, wtok_ref, wvalt_ref,
                   fbias_ref, gq_ref, gk_ref, sel_ref, selt_ref, tri_ref,
                   pq_ref, pk_ref, cq_ref, ck_ref,
                   qf_ref, kf_ref, fvt_ref, sq_ref, sk_ref, svt_ref, gf_ref, gs_ref, fend_ref,
                   carry_scr):
    @pl.when(pl.program_id(1) == 0)
    def _():
        carry_scr[...] = jnp.zeros_like(carry_scr)

    h = _rms_rows(x_ref[0], g_ref[...]).astype(BF16)
    t = fvt_ref.shape[3]
    n_heads = qf_ref.shape[2] // LANES
    width = sq_ref.shape[2]
    d = gf_ref.shape[2]
    c_k, c_sb, c_gate, c_f = width, 2 * width, 4 * width, 4 * width + 2 * d

    yf = _dot(h, wtok_ref[:, c_f:]) + fbias_ref[...]
    yg = _dot(h, wtok_ref[:, c_gate:c_f])

    lf = jnp.minimum(yf, 0.0) - jnp.log1p(jnp.exp(-jnp.abs(yf)))
    lane = lax.broadcasted_iota(jnp.int32, lf.shape, 1)
    lf = jnp.where(lane < n_heads, lf, 0.0)
    c = _dot(tri_ref[...], jnp.concatenate(_split3(lf), axis=1))

    yq = _dot(h, wtok_ref[:, :c_k])
    yk = _dot(h, wtok_ref[:, c_k:c_sb])
    msq = _dot((yq * yq).astype(BF16), sel_ref[...])
    msk = _dot((yk * yk).astype(BF16), sel_ref[...])

    ysb = _dot(h, wtok_ref[:, c_sb:c_gate])

    def spread_rsqrt(ms):
        rhi, rlo = _split2(lax.rsqrt(ms + EPS))
        return _dot(rhi, selt_ref[...]) + _dot(rlo, selt_ref[...])

    rq = spread_rsqrt(msq)
    rk = spread_rsqrt(msk)

    fvt = _dot_nt(wvalt_ref[:width, :], h).astype(BF16)
    svt = _dot_nt(wvalt_ref[width:2 * width, :], h).astype(BF16)

    f_cum = c[:, :LANES] + c[:, LANES:2 * LANES] + c[:, 2 * LANES:] + carry_scr[...]
    carry_scr[...] = f_cum[-1:, :]
    f_bits = f_cum * LOG2E
    fparts = jnp.concatenate(_split3(f_bits), axis=1)
    featq = (_dot(fparts, pq_ref[...]) + cq_ref[...]).astype(BF16)
    featk = (_dot(fparts, pk_ref[...]) + ck_ref[...]).astype(BF16)

    yg = _sigmoid(yg)
    gf_ref[0] = yg[:, :d].astype(BF16)
    gs_ref[0] = yg[:, d:].astype(BF16)

    sq_ref[0] = (ysb[:, :width] * (LOG2E * HEAD_DIM ** -0.5)).astype(BF16)
    sk_ref[0] = ysb[:, width:].astype(BF16)

    for c in range(fvt_ref.shape[1]):
        fvt_ref[0, c] = fvt[:, c * t:(c + 1) * t]
        svt_ref[0, c] = svt[:, c * t:(c + 1) * t]

    fend_ref[0, 0] = jnp.zeros(fend_ref.shape[2:], F32)
    for c in range(fvt_ref.shape[1]):
        fend_ref[0, 0, c:c + 1, :] = f_bits[(c + 1) * t - 1:(c + 1) * t, :]

    q = (yq * rq * gq_ref[...]).astype(BF16)
    k = (yk * rk * gk_ref[...]).astype(BF16)
    for pr in range(q.shape[1] // LANES):
        src = slice(pr * LANES, (pr + 1) * LANES)
        qf_ref[0, :, 2 * pr * LANES:(2 * pr + 1) * LANES] = q[:, src]
        qf_ref[0, :, (2 * pr + 1) * LANES:(2 * pr + 2) * LANES] = featq[:, src]
        kf_ref[0, :, 2 * pr * LANES:(2 * pr + 1) * LANES] = k[:, src]
        kf_ref[0, :, (2 * pr + 1) * LANES:(2 * pr + 2) * LANES] = featk[:, src]


def _inproj(x3d, consts):
    b, s, d = x3d.shape
    tm = TOKEN_TILE
    t = ATTN_TILE
    width = consts[2].shape[0] // 2
    spread = 2 * width
    ins = [x3d] + list(consts)
    in_specs = [pl.BlockSpec((1, tm, d), lambda bi, i: (bi, i, 0))]
    in_specs += [_const_spec(c.shape) for c in consts]

    def tok(n):
        return pl.BlockSpec((1, tm, n), lambda bi, i: (bi, i, 0))

    def shp(n):
        return jax.ShapeDtypeStruct((b, s, n), BF16)

    vt_spec = pl.BlockSpec((1, tm // t, width, t), lambda bi, i: (bi, i, 0, 0))
    vt_shape = jax.ShapeDtypeStruct((b, s // t, width, t), BF16)
    fend_spec = pl.BlockSpec((1, 1, SUBLANES, LANES), lambda bi, i: (bi, i, 0, 0))
    fend_shape = jax.ShapeDtypeStruct((b, s // tm, SUBLANES, LANES), F32)

    return pl.pallas_call(
        _inproj_kernel,
        grid=(b, s // tm),
        in_specs=in_specs,
        out_specs=[tok(spread), tok(spread), vt_spec, tok(width), tok(width), vt_spec,
                   tok(d), tok(d), fend_spec],
        out_shape=[shp(spread), shp(spread), vt_shape, shp(width), shp(width), vt_shape,
                   shp(d), shp(d), fend_shape],
        scratch_shapes=[pltpu.VMEM((1, LANES), F32)],
        compiler_params=_params("parallel", "arbitrary"),
        name="inproj",
    )(*ins)


HEADS_PER_STEP = 8
PAIRS_PER_STEP = HEADS_PER_STEP // 2
N_FEATURES = 6


def _key_query_iota(t):
    return (lax.broadcasted_iota(jnp.int32, (t, t), 0), lax.broadcasted_iota(jnp.int32, (t, t), 1))


def _store_heads(o_ref, outs):
    for pr in range(PAIRS_PER_STEP):
        pair = jnp.concatenate([outs[2 * pr], outs[2 * pr + 1]], axis=0)
        o_ref[0, :, pr * LANES:(pr + 1) * LANES] = pair.T.astype(o_ref.dtype)


def _value_rows(vt_ref, j, h):
    return vt_ref[0, j, h * HEAD_DIM:(h + 1) * HEAD_DIM, :]


def _fox_kernel(fend_ref, margin_ref, qf_ref, kf_ref, vt_ref, o_ref, m_scr, l_scr, acc_scr, s_scr):
    bi = pl.program_id(0)
    hg = pl.program_id(1)
    i = pl.program_id(2)
    t = qf_ref.shape[1]
    heads = range(HEADS_PER_STEP)

    def tile_needed(j):
        need = False
        for h in heads:
            row = (bi * pl.num_programs(1) + hg) * HEADS_PER_STEP + h
            gap = fend_ref[row, j] - fend_ref[row, jnp.maximum(i - 1, 0)]
            need = jnp.logical_or(need, gap <= margin_ref[0])
        return need

    first = lax.while_loop(
        lambda j: jnp.logical_and(j > 0, tile_needed(jnp.maximum(j - 1, 0))), lambda j: j - 1, i)

    key, qry = _key_query_iota(t)
    causal = key <= qry
    lane = lax.broadcasted_iota(jnp.int32, (t, 2 * LANES), 1)
    q = []
    for h in heads:
        qpair = qf_ref[0, :, (h // 2) * 2 * LANES:(h // 2 + 1) * 2 * LANES]
        q_lo, f_lo = (h % 2) * HEAD_DIM, LANES + (h % 2) * N_FEATURES
        own = jnp.logical_or(jnp.logical_and(lane >= q_lo, lane < q_lo + HEAD_DIM),
                             jnp.logical_and(lane >= f_lo, lane < f_lo + N_FEATURES))
        q.append(jnp.where(own, qpair, jnp.zeros_like(qpair)))
    m_scr[...] = jnp.full_like(m_scr, -jnp.inf)
    l_scr[...] = jnp.zeros_like(l_scr)
    acc_scr[...] = jnp.zeros_like(acc_scr)

    ones_rows = jnp.ones((ONES_ROWS, t), BF16)

    def logits(j):
        start = pl.multiple_of(j * t, t)
        return [_dot_nt(kf_ref[0, pl.ds(start, t), (h // 2) * 2 * LANES:(h // 2 + 1) * 2 * LANES], q[h])
                for h in heads]

    def softmax_step(s, j, diagonal):
        if diagonal:
            s = [jnp.where(causal, sh, -jnp.inf) for sh in s]
        m_old = [m_scr[h] for h in heads]
        m_new = [jnp.maximum(m_old[h], jnp.max(s[h], axis=0, keepdims=True)) for h in heads]
        alpha = [jnp.exp2(m_old[h] - m_new[h]) for h in heads]
        p = [jnp.exp2(s[h] - m_new[h]).astype(BF16) for h in heads]
        pv = [_dot(jnp.concatenate([_value_rows(vt_ref, j, h), ones_rows], axis=0), p[h])
              for h in heads]
        for h in heads:
            m_scr[h] = m_new[h]
            l_scr[h] = alpha[h] * l_scr[h] + pv[h][HEAD_DIM:HEAD_DIM + 1, :]
            acc_scr[h] = alpha[h] * acc_scr[h] + pv[h][:HEAD_DIM, :]

    def visit(j, carry):
        s_next = logits(j + 1)
        softmax_step([s_scr[h] for h in heads], j, False)
        for h in heads:
            s_scr[h] = s_next[h]
        return carry

    s_first = logits(first)
    for h in heads:
        s_scr[h] = s_first[h]
    lax.fori_loop(first, i, visit, 0)
    softmax_step([s_scr[h] for h in heads], i, True)
    _store_heads(o_ref, [acc_scr[h] * (1.0 / l_scr[h]) for h in heads])


ONES_ROWS = BF16_ROWS


def _attn_scratch(t):
    n = HEADS_PER_STEP
    return [pltpu.VMEM((n, 1, t), F32), pltpu.VMEM((n, 1, t), F32), pltpu.VMEM((n, HEAD_DIM, t), F32),
            pltpu.VMEM((n, t, t), F32)]


def _fox(fend, margin, qf, kf, fvt):
    b, nblk, width, t = fvt.shape
    s = nblk * t
    vw = PAIRS_PER_STEP * LANES
    qw = HEADS_PER_STEP * LANES
    smem = pl.BlockSpec(memory_space=pltpu.SMEM)
    return pl.pallas_call(
        _fox_kernel,
        grid=(b, width // vw, nblk),
        in_specs=[
            smem, smem,
            pl.BlockSpec((1, t, qw), lambda bi, hg, i: (bi, i, hg)),
            pl.BlockSpec((1, s, qw), lambda bi, hg, i: (bi, 0, hg)),
            pl.BlockSpec((1, nblk, vw, t), lambda bi, hg, i: (bi, 0, hg, 0)),
        ],
        out_specs=pl.BlockSpec((1, t, vw), lambda bi, hg, i: (bi, i, hg)),
        out_shape=jax.ShapeDtypeStruct((b, s, width), BF16),
        scratch_shapes=_attn_scratch(t),
        compiler_params=_params("parallel", "parallel", "arbitrary"),
        name="fox_attention",
    )(fend, margin, qf, kf, fvt)


def _sb_kernel(q_ref, k_ref, vt_ref, u_ref, o_ref, r_scr, acc_scr):
    i = pl.program_id(2)
    t = q_ref.shape[1]
    heads = range(HEADS_PER_STEP)
    key, qry = _key_query_iota(t)
    strict = key < qry
    qlane = lax.broadcasted_iota(jnp.int32, (t, LANES), 1)
    q = []
    for h in heads:
        qpair = q_ref[0, :, (h // 2) * LANES:(h // 2 + 1) * LANES]
        q.append(jnp.where((qlane < HEAD_DIM) == (h % 2 == 0), qpair, jnp.zeros_like(qpair)))
    r_scr[...] = jnp.zeros_like(r_scr)
    acc_scr[...] = jnp.zeros_like(acc_scr)

    def logits(j):
        start = pl.multiple_of(j * t, t)
        return [_dot_nt(k_ref[0, pl.ds(start, t), (h // 2) * LANES:(h // 2 + 1) * LANES], q[h])
                for h in heads]

    def stick_step(z, j, diagonal):
        sp = [jnp.maximum(zh, 0.0) + jnp.log2(1.0 + jnp.exp2(-jnp.abs(zh))) for zh in z]
        log_beta = [z[h] - sp[h] for h in heads]
        if diagonal:
            sp = [jnp.where(strict, sh, 0.0) for sh in sp]
        sums = [_dot(u_ref[...], sh.astype(BF16)) for sh in sp]
        a = [jnp.exp2(log_beta[h] - sums[h][:t, :]) for h in heads]
        if diagonal:
            a = [jnp.where(strict, ah, 0.0) for ah in a]
        pv = [_dot(_value_rows(vt_ref, j, h), a[h].astype(BF16)) for h in heads]
        for h in heads:
            r = r_scr[h]
            acc_scr[h] += jnp.exp2(-r) * pv[h]
            r_scr[h] = r + sums[h][t:t + 1, :]

    def least_r():
        r = r_scr[0]
        for h in heads[1:]:
            r = jnp.minimum(r, r_scr[h])
        return jnp.min(r)

    def body(carry):
        n, _ = carry
        stick_step(logits(i - 1 - n), i - 1 - n, False)
        return n + 1, least_r()

    stick_step(logits(i), i, True)
    lax.while_loop(lambda c: jnp.logical_and(c[0] < i, c[1] < ZERO_WEIGHT_BITS), body,
                   (jnp.int32(0), least_r()))
    _store_heads(o_ref, [acc_scr[h] for h in heads])


def _sb(sq, sk, svt, later_mat):
    b, nblk, width, t = svt.shape
    s = nblk * t
    vw = PAIRS_PER_STEP * LANES
    return pl.pallas_call(
        _sb_kernel,
        grid=(b, width // vw, nblk),
        in_specs=[
            pl.BlockSpec((1, t, vw), lambda bi, hg, i: (bi, i, hg)),
            pl.BlockSpec((1, s, vw), lambda bi, hg, i: (bi, 0, hg)),
            pl.BlockSpec((1, nblk, vw, t), lambda bi, hg, i: (bi, 0, hg, 0)),
            _const_spec(later_mat.shape),
        ],
        out_specs=pl.BlockSpec((1, t, vw), lambda bi, hg, i: (bi, i, hg)),
        out_shape=jax.ShapeDtypeStruct((b, s, width), BF16),
        scratch_shapes=_attn_scratch(t)[1:3],
        compiler_params=_params("parallel", "parallel", "arbitrary"),
        name="sb_attention",
    )(sq, sk, svt, later_mat)


def _post_kernel(x_ref, yf_ref, ys_ref, gf_ref, gs_ref, p_ref,
                 wbf_ref, wbs_ref, wo_ref, g2_ref, wg_ref, wu_ref, wd_ref, gp_ref, wpg_ref, wpp_ref,
                 o_ref):
    merged = (gf_ref[...].astype(F32) * _dot(yf_ref[...], wbf_ref[...])
              + gs_ref[...].astype(F32) * _dot(ys_ref[...], wbs_ref[...]))
    x = x_ref[...] + _dot(merged.astype(BF16), wo_ref[...])

    h = _rms_rows(x, g2_ref[...]).astype(BF16)
    a = _dot(h, wg_ref[...])
    u = _dot(h, wu_ref[...])
    x = x + 0.5 * _dot((a * _sigmoid(a) * u).astype(BF16), wd_ref[...])

    h = _rms_rows(x, gp_ref[...]).astype(BF16)
    gate = _sigmoid(_dot(h, wpg_ref[...]))
    o_ref[...] = x + gate * _dot(p_ref[...].astype(BF16), wpp_ref[...])


def _post(x2d, yf, ys, gf, gs, p2d, weights):
    n, d = x2d.shape
    tm = POST_TILE

    def tok(a):
        return pl.BlockSpec((tm, a.shape[1]), lambda i: (i, 0))

    toks = [x2d, yf, ys, gf, gs, p2d]
    return pl.pallas_call(
        _post_kernel,
        grid=(n // tm,),
        in_specs=[tok(a) for a in toks] + [_const_spec(w.shape) for w in weights],
        out_specs=tok(x2d),
        out_shape=jax.ShapeDtypeStruct((n, d), F32),
        compiler_params=_params("parallel"),
        name="post",
    )(*toks, *weights)


@functools.lru_cache(maxsize=None)
def _layout_constants(n_heads, tile):
    width = n_heads * HEAD_DIM
    sel = np.zeros((width, LANES), np.float32)
    pq = np.zeros((3 * LANES, width), np.float32)
    pk = np.zeros((3 * LANES, width), np.float32)
    cq = np.zeros((1, width), np.float32)
    ck = np.zeros((1, width), np.float32)
    for h in range(n_heads):
        sel[h * HEAD_DIM:(h + 1) * HEAD_DIM, h] = 1.0 / HEAD_DIM
        base = (h // 2) * LANES + (h % 2) * N_FEATURES
        for part in range(3):
            pq[part * LANES + h, base + part] = 1.0
            pk[part * LANES + h, base + 3 + part] = -1.0
            cq[0, base + 3 + part] = 1.0
            ck[0, base + part] = 1.0
    selt = (sel.T > 0).astype(np.float32)
    tri = np.tril(np.ones((tile, tile), np.float32))
    return sel, selt, tri, pq, pk, cq, ck


@functools.lru_cache(maxsize=None)
def _later_matrix(tile):
    return np.concatenate([np.triu(np.ones((tile, tile), np.float32), k=1),
                           np.ones((ONES_ROWS, tile), np.float32)], axis=0)


def kernel(x, p, ffn1_norm, ffn1_w_gate, ffn1_w_up, ffn1_w_down, mix_norm, w_in, forget_bias, q_norm, k_norm, w_branch_fox, w_branch_sb, w_out, ffn2_norm, ffn2_w_gate, ffn2_w_up, ffn2_w_down, ple_norm, w_ple_gate, w_ple_proj):
    b, s, d = x.shape
    depth = w_in.shape[0]
    fox_w = w_branch_fox.shape[1]
    sb_w = w_branch_sb.shape[1]
    n_heads = forget_bias.shape[1]
    assert fox_w == n_heads * HEAD_DIM and sb_w == fox_w and n_heads <= LANES
    assert s % TOKEN_TILE == 0 and TOKEN_TILE % ATTN_TILE == 0 and TOKEN_TILE // ATTN_TILE <= SUBLANES
    assert n_heads % HEADS_PER_STEP == 0 and 2 * N_FEATURES <= LANES
    assert (b * s) % FFN_TILE == 0 and (b * s) % POST_TILE == 0

    sel, selt, tri, pq, pk, cq, ck = _layout_constants(n_heads, TOKEN_TILE)
    later = jnp.asarray(_later_matrix(ATTN_TILE), BF16)

    xf = x.reshape(b * s, d)
    for i in range(depth):
        xf = _ffn(xf, ffn1_norm[i][None], ffn1_w_gate[i].astype(BF16),
                  ffn1_w_up[i].astype(BF16), ffn1_w_down[i].astype(BF16))

        w = w_in[i]
        o_f = 3 * fox_w
        o_sb = o_f + n_heads
        o_gate = o_sb + 3 * sb_w
        wf = jnp.pad(w[:, o_f:o_sb], ((0, 0), (0, LANES - n_heads)))
        w_tok = jnp.concatenate([w[:, :2 * fox_w], w[:, o_sb:o_sb + 2 * sb_w], w[:, o_gate:], wf],
                                axis=1).astype(BF16)
        w_val_t = jnp.concatenate([w[:, 2 * fox_w:o_f], w[:, o_sb + 2 * sb_w:o_gate]],
                                  axis=1).T.astype(BF16)
        fbias = jnp.pad(forget_bias[i][None], ((0, 0), (0, LANES - n_heads)))
        gq = jnp.tile(q_norm[i], n_heads)[None] * (LOG2E * HEAD_DIM ** -0.5)
        gk = jnp.tile(k_norm[i], n_heads)[None]
        consts = [mix_norm[i][None], w_tok, w_val_t, fbias, gq, gk,
                  jnp.asarray(sel, BF16), jnp.asarray(selt, BF16), jnp.asarray(tri, BF16),
                  jnp.asarray(pq, BF16), jnp.asarray(pk, BF16), jnp.asarray(cq), jnp.asarray(ck)]
        qf, kf, fvt, sq, sk, svt, gf, gs, fend = _inproj(xf.reshape(b, s, d), consts)

        fend = fend[:, :, :TOKEN_TILE // ATTN_TILE, :n_heads].reshape(b, s // ATTN_TILE, n_heads)
        fend = fend.transpose(0, 2, 1).reshape(b * n_heads, s // ATTN_TILE)
        ub = LOG2E * HEAD_DIM ** 0.5 * jnp.max(jnp.abs(q_norm[i])) * jnp.max(jnp.abs(k_norm[i]))
        margin = (2.0 * ub + ZERO_WEIGHT_BITS).reshape(1).astype(F32)

        y_fox = _fox(fend, margin, qf, kf, fvt)
        y_sb = _sb(sq, sk, svt, later)

        weights = [w_branch_fox[i].astype(BF16), w_branch_sb[i].astype(BF16), w_out[i].astype(BF16),
                   ffn2_norm[i][None], ffn2_w_gate[i].astype(BF16), ffn2_w_up[i].astype(BF16),
                   ffn2_w_down[i].astype(BF16), ple_norm[i][None],
                   w_ple_gate[i].astype(BF16), w_ple_proj[i].astype(BF16)]
        xf = _post(xf, y_fox.reshape(b * s, fox_w), y_sb.reshape(b * s, sb_w),
                   gf.reshape(b * s, d), gs.reshape(b * s, d), p[i].reshape(b * s, -1), weights)
    return xf.reshape(b, s, d)
```

```python
import functools

import jax
import jax.numpy as jnp
import numpy as np
from jax import lax
from jax.experimental import pallas as pl
from jax.experimental.pallas import tpu as pltpu

F32 = jnp.float32
BF16 = jnp.bfloat16

EPS = 1e-6
HEAD_DIM = 64
LOG2E = 1.4426950408889634
LANES = 128
SUBLANES = 8
BF16_ROWS = 2 * SUBLANES
ZERO_WEIGHT_BITS = 152.0
VMEM_LIMIT_BYTES = 56 * 1024 * 1024

TOKEN_TILE = 512
FFN_TILE = 1024
POST_TILE = 512
ATTN_TILE = 256

NT_DIMS = (((1,), (1,)), ((), ()))


def _dot(a, b):
    return jnp.dot(a, b, preferred_element_type=F32)


def _dot_nt(a, b):
    return lax.dot_general(a, b, NT_DIMS, preferred_element_type=F32)


def _split2(a):
    hi = a.astype(BF16)
    lo = (a - hi.astype(F32)).astype(BF16)
    return hi, lo


def _split3(a):
    p1 = a.astype(BF16)
    r1 = a - p1.astype(F32)
    p2 = r1.astype(BF16)
    r2 = r1 - p2.astype(F32)
    return p1, p2, r2.astype(BF16)


def _rms_rows(x, g):
    ms = jnp.mean(x * x, axis=-1, keepdims=True)
    return x * lax.rsqrt(ms + EPS) * g


def _sigmoid(x):
    return 1.0 / (1.0 + jnp.exp(-x))


def _params(*sem):
    return pltpu.CompilerParams(dimension_semantics=sem, vmem_limit_bytes=VMEM_LIMIT_BYTES)


def _const_spec(shape):
    nd = len(shape)
    return pl.BlockSpec(shape, lambda *_: (0,) * nd, pipeline_mode=pl.Buffered(1))


STAGE_BYTES = 1024 * 1024
STAGE_SLOTS = 4


def _stage_rows(w):
    rows, cols = w.shape
    ch = max(BF16_ROWS, STAGE_BYTES // (4 * cols) // BF16_ROWS * BF16_ROWS)
    while rows % ch:
        ch -= BF16_ROWS
    return ch


def _weight_scratch(weights):
    homes = [pltpu.VMEM(w.shape, BF16) for w in weights]
    stage_shapes = sorted({(_stage_rows(w), w.shape[1]) for w in weights})
    stages = [pltpu.VMEM((STAGE_SLOTS,) + s, F32) for s in stage_shapes]
    return homes, stage_shapes, stages + [pltpu.SemaphoreType.DMA((len(stage_shapes), STAGE_SLOTS))]


def _load_weights(w_hbm, w_vmem, stage_shapes, stages, sem):
    jobs, used = [], [0] * len(stage_shapes)
    for src, dst in zip(w_hbm, w_vmem):
        ch = _stage_rows(src)
        k = stage_shapes.index((ch, src.shape[1]))
        for c in range(src.shape[0] // ch):
            slot = used[k] % STAGE_SLOTS
            used[k] += 1
            copy = pltpu.make_async_copy(src.at[pl.ds(c * ch, ch), :], stages[k].at[slot], sem.at[k, slot])
            jobs.append((copy, dst, c * ch, ch, stages[k], slot))
    ahead = STAGE_SLOTS - 1
    for copy, *_ in jobs[:ahead]:
        copy.start()
    for n, (copy, dst, row, ch, stage, slot) in enumerate(jobs):
        if n + ahead < len(jobs):
            jobs[n + ahead][0].start()
        copy.wait()
        dst[pl.ds(row, ch), :] = stage[slot].astype(BF16)


_HBM = pl.BlockSpec(memory_space=pl.ANY)


def _ffn_kernel(stage_shapes, x_ref, g_ref, wg_hbm, wu_hbm, wd_hbm, o_ref, wg_ref, wu_ref, wd_ref, *dma):
    @pl.when(pl.program_id(0) == 0)
    def _():
        _load_weights([wg_hbm, wu_hbm, wd_hbm], [wg_ref, wu_ref, wd_ref], stage_shapes, dma[:-1], dma[-1])

    x = x_ref[...]
    h = _rms_rows(x, g_ref[...]).astype(BF16)
    a = _dot(h, wg_ref[...])
    u = _dot(h, wu_ref[...])
    act = (a * _sigmoid(a) * u).astype(BF16)
    o_ref[...] = x + 0.5 * _dot(act, wd_ref[...])


def _ffn(x2d, g, wg, wu, wd):
    n, d = x2d.shape
    tm = FFN_TILE
    homes, stage_shapes, dma = _weight_scratch([wg, wu, wd])
    return pl.pallas_call(
        functools.partial(_ffn_kernel, stage_shapes),
        grid=(n // tm,),
        in_specs=[pl.BlockSpec((tm, d), lambda i: (i, 0)), _const_spec(g.shape), _HBM, _HBM, _HBM],
        out_specs=pl.BlockSpec((tm, d), lambda i: (i, 0)),
        out_shape=jax.ShapeDtypeStruct((n, d), F32),
        scratch_shapes=homes + dma,
        compiler_params=_params("arbitrary"),
        name="ffn",
    )(x2d, g, wg, wu, wd)


def _inproj_kernel(x_ref, g_ref, wtok_ref, wvalt_ref,
                   fbias_ref, gq_ref, gk_ref, sel_ref, selt_ref, tri_ref,
                   pq_ref, pk_ref, cq_ref, ck_ref,
                   qf_ref, kf_ref, fvt_ref, sq_ref, sk_ref, svt_ref, gf_ref, gs_ref, fend_ref,
                   carry_scr):
    @pl.when(pl.program_id(1) == 0)
    def _():
        carry_scr[...] = jnp.zeros_like(carry_scr)

    h = _rms_rows(x_ref[0], g_ref[...]).astype(BF16)
    t = fvt_ref.shape[3]
    n_heads = qf_ref.shape[2] // LANES
    width = sq_ref.shape[2]
    d = gf_ref.shape[2]
    c_k, c_sb, c_gate, c_f = width, 2 * width, 4 * width, 4 * width + 2 * d

    yf = _dot(h, wtok_ref[:, c_f:]) + fbias_ref[...]
    yg = _dot(h, wtok_ref[:, c_gate:c_f])

    lf = jnp.minimum(yf, 0.0) - jnp.log1p(jnp.exp(-jnp.abs(yf)))
    lane = lax.broadcasted_iota(jnp.int32, lf.shape, 1)
    lf = jnp.where(lane < n_heads, lf, 0.0)
    c = _dot(tri_ref[...], jnp.concatenate(_split3(lf), axis=1))

    yq = _dot(h, wtok_ref[:, :c_k])
    yk = _dot(h, wtok_ref[:, c_k:c_sb])
    msq = _dot((yq * yq).astype(BF16), sel_ref[...])
    msk = _dot((yk * yk).astype(BF16), sel_ref[...])

    ysb = _dot(h, wtok_ref[:, c_sb:c_gate])

    def spread_rsqrt(ms):
        rhi, rlo = _split2(lax.rsqrt(ms + EPS))
        return _dot(rhi, selt_ref[...]) + _dot(rlo, selt_ref[...])

    rq = spread_rsqrt(msq)
    rk = spread_rsqrt(msk)

    fvt = _dot_nt(wvalt_ref[:width, :], h).astype(BF16)
    svt = _dot_nt(wvalt_ref[width:2 * width, :], h).astype(BF16)

    f_cum = c[:, :LANES] + c[:, LANES:2 * LANES] + c[:, 2 * LANES:] + carry_scr[...]
    carry_scr[...] = f_cum[-1:, :]
    f_bits = f_cum * LOG2E
    fparts = jnp.concatenate(_split3(f_bits), axis=1)
    featq = (_dot(fparts, pq_ref[...]) + cq_ref[...]).astype(BF16)
    featk = (_dot(fparts, pk_ref[...]) + ck_ref[...]).astype(BF16)

    yg = _sigmoid(yg)
    gf_ref[0] = yg[:, :d].astype(BF16)
    gs_ref[0] = yg[:, d:].astype(BF16)

    sq_ref[0] = (ysb[:, :width] * (LOG2E * HEAD_DIM ** -0.5)).astype(BF16)
    sk_ref[0] = ysb[:, width:].astype(BF16)

    for c in range(fvt_ref.shape[1]):
        fvt_ref[0, c] = fvt[:, c * t:(c + 1) * t]
        svt_ref[0, c] = svt[:, c * t:(c + 1) * t]

    fend_ref[0, 0] = jnp.zeros(fend_ref.shape[2:], F32)
    for c in range(fvt_ref.shape[1]):
        fend_ref[0, 0, c:c + 1, :] = f_bits[(c + 1) * t - 1:(c + 1) * t, :]

    q = (yq * rq * gq_ref[...]).astype(BF16)
    k = (yk * rk * gk_ref[...]).astype(BF16)
    for pr in range(q.shape[1] // LANES):
        src = slice(pr * LANES, (pr + 1) * LANES)
        qf_ref[0, :, 2 * pr * LANES:(2 * pr + 1) * LANES] = q[:, src]
        qf_ref[0, :, (2 * pr + 1) * LANES:(2 * pr + 2) * LANES] = featq[:, src]
        kf_ref[0, :, 2 * pr * LANES:(2 * pr + 1) * LANES] = k[:, src]
        kf_ref[0, :, (2 * pr + 1) * LANES:(2 * pr + 2) * LANES] = featk[:, src]


def _inproj(x3d, consts):
    b, s, d = x3d.shape
    tm = TOKEN_TILE
    t = ATTN_TILE
    width = consts[2].shape[0] // 2
    spread = 2 * width
    ins = [x3d] + list(consts)
    in_specs = [pl.BlockSpec((1, tm, d), lambda bi, i: (bi, i, 0))]
    in_specs += [_const_spec(c.shape) for c in consts]

    def tok(n):
        return pl.BlockSpec((1, tm, n), lambda bi, i: (bi, i, 0))

    def shp(n):
        return jax.ShapeDtypeStruct((b, s, n), BF16)

    vt_spec = pl.BlockSpec((1, tm // t, width, t), lambda bi, i: (bi, i, 0, 0))
    vt_shape = jax.ShapeDtypeStruct((b, s // t, width, t), BF16)
    fend_spec = pl.BlockSpec((1, 1, SUBLANES, LANES), lambda bi, i: (bi, i, 0, 0))
    fend_shape = jax.ShapeDtypeStruct((b, s // tm, SUBLANES, LANES), F32)

    return pl.pallas_call(
        _inproj_kernel,
        grid=(b, s // tm),
        in_specs=in_specs,
        out_specs=[tok(spread), tok(spread), vt_spec, tok(width), tok(width), vt_spec,
                   tok(d), tok(d), fend_spec],
        out_shape=[shp(spread), shp(spread), vt_shape, shp(width), shp(width), vt_shape,
                   shp(d), shp(d), fend_shape],
        scratch_shapes=[pltpu.VMEM((1, LANES), F32)],
        compiler_params=_params("parallel", "arbitrary"),
        name="inproj",
    )(*ins)


HEADS_PER_STEP = 8
PAIRS_PER_STEP = HEADS_PER_STEP // 2
N_FEATURES = 6


def _key_query_iota(t):
    return (lax.broadcasted_iota(jnp.int32, (t, t), 0), lax.broadcasted_iota(jnp.int32, (t, t), 1))


def _store_heads(o_ref, outs):
    for pr in range(PAIRS_PER_STEP):
        pair = jnp.concatenate([outs[2 * pr], outs[2 * pr + 1]], axis=0)
        o_ref[0, :, pr * LANES:(pr + 1) * LANES] = pair.T.astype(o_ref.dtype)


def _value_rows(vt_ref, j, h):
    return vt_ref[0, j, h * HEAD_DIM:(h + 1) * HEAD_DIM, :]


def _fox_kernel(fend_ref, margin_ref, qf_ref, kf_ref, vt_ref, o_ref, m_scr, l_scr, acc_scr, s_scr):
    bi = pl.program_id(0)
    hg = pl.program_id(1)
    i = pl.program_id(2)
    t = qf_ref.shape[1]
    heads = range(HEADS_PER_STEP)

    def tile_needed(j):
        need = False
        for h in heads:
            row = (bi * pl.num_programs(1) + hg) * HEADS_PER_STEP + h
            gap = fend_ref[row, j] - fend_ref[row, jnp.maximum(i - 1, 0)]
            need = jnp.logical_or(need, gap <= margin_ref[0])
        return need

    first = lax.while_loop(
        lambda j: jnp.logical_and(j > 0, tile_needed(jnp.maximum(j - 1, 0))), lambda j: j - 1, i)

    key, qry = _key_query_iota(t)
    causal = key <= qry
    lane = lax.broadcasted_iota(jnp.int32, (t, 2 * LANES), 1)
    q = []
    for h in heads:
        qpair = qf_ref[0, :, (h // 2) * 2 * LANES:(h // 2 + 1) * 2 * LANES]
        q_lo, f_lo = (h % 2) * HEAD_DIM, LANES + (h % 2) * N_FEATURES
        own = jnp.logical_or(jnp.logical_and(lane >= q_lo, lane < q_lo + HEAD_DIM),
                             jnp.logical_and(lane >= f_lo, lane < f_lo + N_FEATURES))
        q.append(jnp.where(own, qpair, jnp.zeros_like(qpair)))
    m_scr[...] = jnp.full_like(m_scr, -jnp.inf)
    l_scr[...] = jnp.zeros_like(l_scr)
    acc_scr[...] = jnp.zeros_like(acc_scr)

    ones_rows = jnp.ones((ONES_ROWS, t), BF16)

    def logits(j):
        start = pl.multiple_of(j * t, t)
        return [_dot_nt(kf_ref[0, pl.ds(start, t), (h // 2) * 2 * LANES:(h // 2 + 1) * 2 * LANES], q[h])
                for h in heads]

    def softmax_step(s, j, diagonal):
        if diagonal:
            s = [jnp.where(causal, sh, -jnp.inf) for sh in s]
        m_old = [m_scr[h] for h in heads]
        m_new = [jnp.maximum(m_old[h], jnp.max(s[h], axis=0, keepdims=True)) for h in heads]
        alpha = [jnp.exp2(m_old[h] - m_new[h]) for h in heads]
        p = [jnp.exp2(s[h] - m_new[h]).astype(BF16) for h in heads]
        pv = [_dot(jnp.concatenate([_value_rows(vt_ref, j, h), ones_rows], axis=0), p[h])
              for h in heads]
        for h in heads:
            m_scr[h] = m_new[h]
            l_scr[h] = alpha[h] * l_scr[h] + pv[h][HEAD_DIM:HEAD_DIM + 1, :]
            acc_scr[h] = alpha[h] * acc_scr[h] + pv[h][:HEAD_DIM, :]

    def visit(j, carry):
        s_next = logits(j + 1)
        softmax_step([s_scr[h] for h in heads], j, False)
        for h in heads:
            s_scr[h] = s_next[h]
        return carry

    s_first = logits(first)
    for h in heads:
        s_scr[h] = s_first[h]
    lax.fori_loop(first, i, visit, 0)
    softmax_step([s_scr[h] for h in heads], i, True)
    _store_heads(o_ref, [acc_scr[h] * (1.0 / l_scr[h]) for h in heads])


ONES_ROWS = BF16_ROWS


def _attn_scratch(t):
    n = HEADS_PER_STEP
    return [pltpu.VMEM((n, 1, t), F32), pltpu.VMEM((n, 1, t), F32), pltpu.VMEM((n, HEAD_DIM, t), F32),
            pltpu.VMEM((n, t, t), F32)]


def _fox(fend, margin, qf, kf, fvt):
    b, nblk, width, t = fvt.shape
    s = nblk * t
    vw = PAIRS_PER_STEP * LANES
    qw = HEADS_PER_STEP * LANES
    smem = pl.BlockSpec(memory_space=pltpu.SMEM)
    return pl.pallas_call(
        _fox_kernel,
        grid=(b, width // vw, nblk),
        in_specs=[
            smem, smem,
            pl.BlockSpec((1, t, qw), lambda bi, hg, i: (bi, i, hg)),
            pl.BlockSpec((1, s, qw), lambda bi, hg, i: (bi, 0, hg)),
            pl.BlockSpec((1, nblk, vw, t), lambda bi, hg, i: (bi, 0, hg, 0)),
        ],
        out_specs=pl.BlockSpec((1, t, vw), lambda bi, hg, i: (bi, i, hg)),
        out_shape=jax.ShapeDtypeStruct((b, s, width), BF16),
        scratch_shapes=_attn_scratch(t),
        compiler_params=_params("parallel", "parallel", "arbitrary"),
        name="fox_attention",
    )(fend, margin, qf, kf, fvt)


def _sb_kernel(q_ref, k_ref, vt_ref, u_ref, o_ref, r_scr, acc_scr):
    i = pl.program_id(2)
    t = q_ref.shape[1]
    heads = range(HEADS_PER_STEP)
    key, qry = _key_query_iota(t)
    strict = key < qry
    qlane = lax.broadcasted_iota(jnp.int32, (t, LANES), 1)
    q = []
    for h in heads:
        qpair = q_ref[0, :, (h // 2) * LANES:(h // 2 + 1) * LANES]
        q.append(jnp.where((qlane < HEAD_DIM) == (h % 2 == 0), qpair, jnp.zeros_like(qpair)))
    r_scr[...] = jnp.zeros_like(r_scr)
    acc_scr[...] = jnp.zeros_like(acc_scr)

    def logits(j):
        start = pl.multiple_of(j * t, t)
        return [_dot_nt(k_ref[0, pl.ds(start, t), (h // 2) * LANES:(h // 2 + 1) * LANES], q[h])
                for h in heads]

    def stick_step(z, j, diagonal):
        sp = [jnp.maximum(zh, 0.0) + jnp.log2(1.0 + jnp.exp2(-jnp.abs(zh))) for zh in z]
        log_beta = [z[h] - sp[h] for h in heads]
        if diagonal:
            sp = [jnp.where(strict, sh, 0.0) for sh in sp]
        sums = [_dot(u_ref[...], sh.astype(BF16)) for sh in sp]
        a = [jnp.exp2(log_beta[h] - sums[h][:t, :]) for h in heads]
        if diagonal:
            a = [jnp.where(strict, ah, 0.0) for ah in a]
        pv = [_dot(_value_rows(vt_ref, j, h), a[h].astype(BF16)) for h in heads]
        for h in heads:
            r = r_scr[h]
            acc_scr[h] += jnp.exp2(-r) * pv[h]
            r_scr[h] = r + sums[h][t:t + 1, :]

    def least_r():
        r = r_scr[0]
        for h in heads[1:]:
            r = jnp.minimum(r, r_scr[h])
        return jnp.min(r)

    def body(carry):
        n, _ = carry
        stick_step(logits(i - 1 - n), i - 1 - n, False)
        return n + 1, least_r()

    stick_step(logits(i), i, True)
    lax.while_loop(lambda c: jnp.logical_and(c[0] < i, c[1] < ZERO_WEIGHT_BITS), body,
                   (jnp.int32(0), least_r()))
    _store_heads(o_ref, [acc_scr[h] for h in heads])


def _sb(sq, sk, svt, later_mat):
    b, nblk, width, t = svt.shape
    s = nblk * t
    vw = PAIRS_PER_STEP * LANES
    return pl.pallas_call(
        _sb_kernel,
        grid=(b, width // vw, nblk),
        in_specs=[
            pl.BlockSpec((1, t, vw), lambda bi, hg, i: (bi, i, hg)),
            pl.BlockSpec((1, s, vw), lambda bi, hg, i: (bi, 0, hg)),
            pl.BlockSpec((1, nblk, vw, t), lambda bi, hg, i: (bi, 0, hg, 0)),
            _const_spec(later_mat.shape),
        ],
        out_specs=pl.BlockSpec((1, t, vw), lambda bi, hg, i: (bi, i, hg)),
        out_shape=jax.ShapeDtypeStruct((b, s, width), BF16),
        scratch_shapes=_attn_scratch(t)[1:3],
        compiler_params=_params("parallel", "parallel", "arbitrary"),
        name="sb_attention",
    )(sq, sk, svt, later_mat)


N_POST_WEIGHTS = 8


def _post_kernel(stage_shapes, x_ref, yf_ref, ys_ref, gf_ref, gs_ref, p_ref, g2_ref, gp_ref, *rest):
    w_hbm, o_ref = rest[:N_POST_WEIGHTS], rest[N_POST_WEIGHTS]
    w_vmem = rest[N_POST_WEIGHTS + 1:2 * N_POST_WEIGHTS + 1]
    dma = rest[2 * N_POST_WEIGHTS + 1:]

    @pl.when(pl.program_id(0) == 0)
    def _():
        _load_weights(w_hbm, w_vmem, stage_shapes, dma[:-1], dma[-1])

    wbf_ref, wbs_ref, wo_ref, wg_ref, wu_ref, wd_ref, wpg_ref, wpp_ref = w_vmem
    merged =(gf_ref[...].astype(F32) * _dot(yf_ref[...], wbf_ref[...])
              + gs_ref[...].astype(F32) * _dot(ys_ref[...], wbs_ref[...]))
    x = x_ref[...] + _dot(merged.astype(BF16), wo_ref[...])

    h = _rms_rows(x, g2_ref[...]).astype(BF16)
    a = _dot(h, wg_ref[...])
    u = _dot(h, wu_ref[...])
    x = x + 0.5 * _dot((a * _sigmoid(a) * u).astype(BF16), wd_ref[...])

    h = _rms_rows(x, gp_ref[...]).astype(BF16)
    gate = _sigmoid(_dot(h, wpg_ref[...]))
    o_ref[...] = x + gate * _dot(p_ref[...].astype(BF16), wpp_ref[...])


def _post(x2d, yf, ys, gf, gs, p2d, gains, weights):
    n, d = x2d.shape
    tm = POST_TILE
    assert len(weights) == N_POST_WEIGHTS

    def tok(a):
        return pl.BlockSpec((tm, a.shape[1]), lambda i: (i, 0))

    toks = [x2d, yf, ys, gf, gs, p2d]
    homes, stage_shapes, dma = _weight_scratch(weights)
    return pl.pallas_call(
        functools.partial(_post_kernel, stage_shapes),
        grid=(n // tm,),
        in_specs=[tok(a) for a in toks] + [_const_spec(g.shape) for g in gains] + [_HBM] * len(weights),
        out_specs=tok(x2d),
        out_shape=jax.ShapeDtypeStruct((n, d), F32),
        scratch_shapes=homes + dma,
        compiler_params=_params("arbitrary"),
        name="post",
    )(*toks, *gains, *weights)


@functools.lru_cache(maxsize=None)
def _layout_constants(n_heads, tile):
    width = n_heads * HEAD_DIM
    sel = np.zeros((width, LANES), np.float32)
    pq = np.zeros((3 * LANES, width), np.float32)
    pk = np.zeros((3 * LANES, width), np.float32)
    cq = np.zeros((1, width), np.float32)
    ck = np.zeros((1, width), np.float32)
    for h in range(n_heads):
        sel[h * HEAD_DIM:(h + 1) * HEAD_DIM, h] = 1.0 / HEAD_DIM
        base = (h // 2) * LANES + (h % 2) * N_FEATURES
        for part in range(3):
            pq[part * LANES + h, base + part] = 1.0
            pk[part * LANES + h, base + 3 + part] = -1.0
            cq[0, base + 3 + part] = 1.0
            ck[0, base + part] = 1.0
    selt = (sel.T > 0).astype(np.float32)
    tri = np.tril(np.ones((tile, tile), np.float32))
    return sel, selt, tri, pq, pk, cq, ck


@functools.lru_cache(maxsize=None)
def _later_matrix(tile):
    return np.concatenate([np.triu(np.ones((tile, tile), np.float32), k=1),
                           np.ones((ONES_ROWS, tile), np.float32)], axis=0)


def kernel(x, p, ffn1_norm, ffn1_w_gate, ffn1_w_up, ffn1_w_down, mix_norm, w_in, forget_bias, q_norm, k_norm, w_branch_fox, w_branch_sb, w_out, ffn2_norm, ffn2_w_gate, ffn2_w_up, ffn2_w_down, ple_norm, w_ple_gate, w_ple_proj):
    b, s, d = x.shape
    depth = w_in.shape[0]
    fox_w = w_branch_fox.shape[1]
    sb_w = w_branch_sb.shape[1]
    n_heads = forget_bias.shape[1]
    assert fox_w == n_heads * HEAD_DIM and sb_w == fox_w and n_heads <= LANES
    assert s % TOKEN_TILE == 0 and TOKEN_TILE % ATTN_TILE == 0 and TOKEN_TILE // ATTN_TILE <= SUBLANES
    assert n_heads % HEADS_PER_STEP == 0 and 2 * N_FEATURES <= LANES
    assert (b * s) % FFN_TILE == 0 and (b * s) % POST_TILE == 0

    sel, selt, tri, pq, pk, cq, ck = _layout_constants(n_heads, TOKEN_TILE)
    later = jnp.asarray(_later_matrix(ATTN_TILE), BF16)

    xf = x.reshape(b * s, d)
    for i in range(depth):
        xf = _ffn(xf, ffn1_norm[i][None], ffn1_w_gate[i], ffn1_w_up[i], ffn1_w_down[i])

        w = w_in[i]
        o_f = 3 * fox_w
        o_sb = o_f + n_heads
        o_gate = o_sb + 3 * sb_w
        wf = jnp.pad(w[:, o_f:o_sb], ((0, 0), (0, LANES - n_heads)))
        w_tok = jnp.concatenate([w[:, :2 * fox_w], w[:, o_sb:o_sb + 2 * sb_w], w[:, o_gate:], wf],
                                axis=1).astype(BF16)
        w_val_t = jnp.concatenate([w[:, 2 * fox_w:o_f], w[:, o_sb + 2 * sb_w:o_gate]],
                                  axis=1).T.astype(BF16)
        fbias = jnp.pad(forget_bias[i][None], ((0, 0), (0, LANES - n_heads)))
        gq = jnp.tile(q_norm[i], n_heads)[None] * (LOG2E * HEAD_DIM ** -0.5)
        gk = jnp.tile(k_norm[i], n_heads)[None]
        consts = [mix_norm[i][None], w_tok, w_val_t, fbias, gq, gk,
                  jnp.asarray(sel, BF16), jnp.asarray(selt, BF16), jnp.asarray(tri, BF16),
                  jnp.asarray(pq, BF16), jnp.asarray(pk, BF16), jnp.asarray(cq), jnp.asarray(ck)]
        qf, kf, fvt, sq, sk, svt, gf, gs, fend = _inproj(xf.reshape(b, s, d), consts)

        fend = fend[:, :, :TOKEN_TILE // ATTN_TILE, :n_heads].reshape(b, s // ATTN_TILE, n_heads)
        fend = fend.transpose(0, 2, 1).reshape(b * n_heads, s // ATTN_TILE)
        ub = LOG2E * HEAD_DIM ** 0.5 * jnp.max(jnp.abs(q_norm[i])) * jnp.max(jnp.abs(k_norm[i]))
        margin = (2.0 * ub + ZERO_WEIGHT_BITS).reshape(1).astype(F32)

        y_fox = _fox(fend, margin, qf, kf, fvt)
        y_sb = _sb(sq, sk, svt, later)

        weights = [w_branch_fox[i], w_branch_sb[i], w_out[i], ffn2_w_gate[i], ffn2_w_up[i],
                   ffn2_w_down[i], w_ple_gate[i], w_ple_proj[i]]
        xf = _post(xf, y_fox.reshape(b * s, fox_w), y_sb.reshape(b * s, sb_w),
                   gf.reshape(b * s, d), gs.reshape(b * s, d), p[i].reshape(b * s, -1),
                   [ffn2_norm[i][None], ple_norm[i][None]], weights)
    return xf.reshape(b, s, d)
```

```python
import functools

import jax
import jax.numpy as jnp
import numpy as np
from jax import lax
from jax.experimental import pallas as pl
from jax.experimental.pallas import tpu as pltpu

F32 = jnp.float32
BF16 = jnp.bfloat16

EPS = 1e-6
HEAD_DIM = 64
LOG2E = 1.4426950408889634
LANES = 128
SUBLANES = 8
BF16_ROWS = 2 * SUBLANES
ZERO_WEIGHT_BITS = 152.0
VMEM_LIMIT_BYTES = 56 * 1024 * 1024

TOKEN_TILE = 512
FFN_TILE = 1024
POST_TILE = 512
ATTN_TILE = 256

NT_DIMS = (((1,), (1,)), ((), ()))


def _dot(a, b):
    return jnp.dot(a, b, preferred_element_type=F32)


def _dot_nt(a, b):
    return lax.dot_general(a, b, NT_DIMS, preferred_element_type=F32)


def _split2(a):
    hi = a.astype(BF16)
    lo = (a - hi.astype(F32)).astype(BF16)
    return hi, lo


def _split3(a):
    p1 = a.astype(BF16)
    r1 = a - p1.astype(F32)
    p2 = r1.astype(BF16)
    r2 = r1 - p2.astype(F32)
    return p1, p2, r2.astype(BF16)


def _rms_rows(x, g):
    ms = jnp.mean(x * x, axis=-1, keepdims=True)
    return x * lax.rsqrt(ms + EPS) * g


def _sigmoid(x):
    return 1.0 / (1.0 + jnp.exp(-x))


def _params(*sem):
    return pltpu.CompilerParams(dimension_semantics=sem, vmem_limit_bytes=VMEM_LIMIT_BYTES)


def _const_spec(shape):
    nd = len(shape)
    return pl.BlockSpec(shape, lambda *_: (0,) * nd, pipeline_mode=pl.Buffered(1))


STAGE_BYTES = 1024 * 1024
STAGE_SLOTS = 4


def _stage_rows(w):
    rows, cols = w.shape
    ch = max(BF16_ROWS, STAGE_BYTES // (4 * cols) // BF16_ROWS * BF16_ROWS)
    while rows % ch:
        ch -= BF16_ROWS
    return ch


def _weight_scratch(weights):
    homes = [pltpu.VMEM(w.shape, BF16) for w in weights]
    stage_shapes = sorted({(_stage_rows(w), w.shape[1]) for w in weights})
    stages = [pltpu.VMEM((STAGE_SLOTS,) + s, F32) for s in stage_shapes]
    return homes, stage_shapes, stages + [pltpu.SemaphoreType.DMA((len(stage_shapes), STAGE_SLOTS))]


def _load_weights(w_hbm, w_vmem, stage_shapes, stages, sem):
    jobs, used = [], [0] * len(stage_shapes)
    for src, dst in zip(w_hbm, w_vmem):
        ch = _stage_rows(src)
        k = stage_shapes.index((ch, src.shape[1]))
        for c in range(src.shape[0] // ch):
            slot = used[k] % STAGE_SLOTS
            used[k] += 1
            copy = pltpu.make_async_copy(src.at[pl.ds(c * ch, ch), :], stages[k].at[slot], sem.at[k, slot])
            jobs.append((copy, dst, c * ch, ch, stages[k], slot))
    ahead = STAGE_SLOTS - 1
    for copy, *_ in jobs[:ahead]:
        copy.start()
    for n, (copy, dst, row, ch, stage, slot) in enumerate(jobs):
        if n + ahead < len(jobs):
            jobs[n + ahead][0].start()
        copy.wait()
        dst[pl.ds(row, ch), :] = stage[slot].astype(BF16)


_HBM = pl.BlockSpec(memory_space=pl.ANY)


def _ffn_kernel(stage_shapes, x_ref, g_ref, wg_hbm, wu_hbm, wd_hbm, o_ref, wg_ref, wu_ref, wd_ref, *dma):
    @pl.when(pl.program_id(0) == 0)
    def _():
        _load_weights([wg_hbm, wu_hbm, wd_hbm], [wg_ref, wu_ref, wd_ref], stage_shapes, dma[:-1], dma[-1])

    x = x_ref[...]
    h = _rms_rows(x, g_ref[...]).astype(BF16)
    a = _dot(h, wg_ref[...])
    u = _dot(h, wu_ref[...])
    act = (a * _sigmoid(a) * u).astype(BF16)
    o_ref[...] = x + 0.5 * _dot(act, wd_ref[...])


def _ffn(x2d, g, wg, wu, wd):
    n, d = x2d.shape
    tm = FFN_TILE
    homes, stage_shapes, dma = _weight_scratch([wg, wu, wd])
    return pl.pallas_call(
        functools.partial(_ffn_kernel, stage_shapes),
        grid=(n // tm,),
        in_specs=[pl.BlockSpec((tm, d), lambda i: (i, 0)), _const_spec(g.shape), _HBM, _HBM, _HBM],
        out_specs=pl.BlockSpec((tm, d), lambda i: (i, 0)),
        out_shape=jax.ShapeDtypeStruct((n, d), F32),
        scratch_shapes=homes + dma,
        compiler_params=_params("arbitrary"),
        name="ffn",
    )(x2d, g, wg, wu, wd)


def _inproj_kernel(x_ref, g_ref, wtok_ref, wval_ref,
                   fbias_ref, gq_ref, gk_ref, sel_ref, selt_ref, tri_ref,
                   pq_ref, pk_ref, cq_ref, ck_ref,
                   qf_ref, kf_ref, fvt_ref, sq_ref, sk_ref, svt_ref, gf_ref, gs_ref, fend_ref,
                   carry_scr, wvalt_ref):
    @pl.when(pl.program_id(1) == 0)
    def _():
        carry_scr[...] = jnp.zeros_like(carry_scr)

    @pl.when(jnp.logical_and(pl.program_id(0) == 0, pl.program_id(1) == 0))
    def _():
        wvalt_ref[...] = wval_ref[...].astype(F32).T.astype(BF16)

    h = _rms_rows(x_ref[0], g_ref[...]).astype(BF16)
    t = fvt_ref.shape[3]
    n_heads = qf_ref.shape[2] // LANES
    width = sq_ref.shape[2]
    d = gf_ref.shape[2]
    c_k, c_sb, c_gate, c_f = width, 2 * width, 4 * width, 4 * width + 2 * d

    yf = _dot(h, wtok_ref[:, c_f:]) + fbias_ref[...]
    yg = _dot(h, wtok_ref[:, c_gate:c_f])

    lf = jnp.minimum(yf, 0.0) - jnp.log1p(jnp.exp(-jnp.abs(yf)))
    lane = lax.broadcasted_iota(jnp.int32, lf.shape, 1)
    lf = jnp.where(lane < n_heads, lf, 0.0)
    c = _dot(tri_ref[...], jnp.concatenate(_split3(lf), axis=1))

    yq = _dot(h, wtok_ref[:, :c_k])
    yk = _dot(h, wtok_ref[:, c_k:c_sb])
    msq = _dot((yq * yq).astype(BF16), sel_ref[...])
    msk = _dot((yk * yk).astype(BF16), sel_ref[...])

    ysb = _dot(h, wtok_ref[:, c_sb:c_gate])

    def spread_rsqrt(ms):
        rhi, rlo = _split2(lax.rsqrt(ms + EPS))
        return _dot(rhi, selt_ref[...]) + _dot(rlo, selt_ref[...])

    rq = spread_rsqrt(msq)
    rk = spread_rsqrt(msk)

    fvt = _dot_nt(wvalt_ref[:width, :], h).astype(BF16)
    svt = _dot_nt(wvalt_ref[width:2 * width, :], h).astype(BF16)

    f_cum = c[:, :LANES] + c[:, LANES:2 * LANES] + c[:, 2 * LANES:] + carry_scr[...]
    carry_scr[...] = f_cum[-1:, :]
    f_bits = f_cum * LOG2E
    fparts = jnp.concatenate(_split3(f_bits), axis=1)
    featq = (_dot(fparts, pq_ref[...]) + cq_ref[...]).astype(BF16)
    featk = (_dot(fparts, pk_ref[...]) + ck_ref[...]).astype(BF16)

    yg = _sigmoid(yg)
    gf_ref[0] = yg[:, :d].astype(BF16)
    gs_ref[0] = yg[:, d:].astype(BF16)

    sq_ref[0] = (ysb[:, :width] * (LOG2E * HEAD_DIM ** -0.5)).astype(BF16)
    sk_ref[0] = ysb[:, width:].astype(BF16)

    for c in range(fvt_ref.shape[1]):
        fvt_ref[0, c] = fvt[:, c * t:(c + 1) * t]
        svt_ref[0, c] = svt[:, c * t:(c + 1) * t]

    fend_ref[0, 0] = jnp.zeros(fend_ref.shape[2:], F32)
    for c in range(fvt_ref.shape[1]):
        fend_ref[0, 0, c:c + 1, :] = f_bits[(c + 1) * t - 1:(c + 1) * t, :]

    q = (yq * rq * gq_ref[...]).astype(BF16)
    k = (yk * rk * gk_ref[...]).astype(BF16)
    for pr in range(q.shape[1] // LANES):
        src = slice(pr * LANES, (pr + 1) * LANES)
        qf_ref[0, :, 2 * pr * LANES:(2 * pr + 1) * LANES] = q[:, src]
        qf_ref[0, :, (2 * pr + 1) * LANES:(2 * pr + 2) * LANES] = featq[:, src]
        kf_ref[0, :, 2 * pr * LANES:(2 * pr + 1) * LANES] = k[:, src]
        kf_ref[0, :, (2 * pr + 1) * LANES:(2 * pr + 2) * LANES] = featk[:, src]


def _inproj(x3d, consts):
    b, s, d = x3d.shape
    tm = TOKEN_TILE
    t = ATTN_TILE
    width = consts[2].shape[1] // 2
    spread = 2 * width
    ins = [x3d] + list(consts)
    in_specs = [pl.BlockSpec((1, tm, d), lambda bi, i: (bi, i, 0))]
    in_specs += [_const_spec(c.shape) for c in consts]

    def tok(n):
        return pl.BlockSpec((1, tm, n), lambda bi, i: (bi, i, 0))

    def shp(n):
        return jax.ShapeDtypeStruct((b, s, n), BF16)

    vt_spec = pl.BlockSpec((1, tm // t, width, t), lambda bi, i: (bi, i, 0, 0))
    vt_shape = jax.ShapeDtypeStruct((b, s // t, width, t), BF16)
    fend_spec = pl.BlockSpec((1, 1, SUBLANES, LANES), lambda bi, i: (bi, i, 0, 0))
    fend_shape = jax.ShapeDtypeStruct((b, s // tm, SUBLANES, LANES), F32)

    return pl.pallas_call(
        _inproj_kernel,
        grid=(b, s // tm),
        in_specs=in_specs,
        out_specs=[tok(spread), tok(spread), vt_spec, tok(width), tok(width), vt_spec,
                   tok(d), tok(d), fend_spec],
        out_shape=[shp(spread), shp(spread), vt_shape, shp(width), shp(width), vt_shape,
                   shp(d), shp(d), fend_shape],
        scratch_shapes=[pltpu.VMEM((1, LANES), F32), pltpu.VMEM(consts[2].shape[::-1], BF16)],
        compiler_params=_params("arbitrary", "arbitrary"),
        name="inproj",
    )(*ins)


HEADS_PER_STEP = 8
PAIRS_PER_STEP = HEADS_PER_STEP // 2
N_FEATURES = 6


def _key_query_iota(t):
    return (lax.broadcasted_iota(jnp.int32, (t, t), 0), lax.broadcasted_iota(jnp.int32, (t, t), 1))


def _store_heads(o_ref, outs):
    for pr in range(PAIRS_PER_STEP):
        pair = jnp.concatenate([outs[2 * pr], outs[2 * pr + 1]], axis=0)
        o_ref[0, :, pr * LANES:(pr + 1) * LANES] = pair.T.astype(o_ref.dtype)


def _value_rows(vt_ref, j, h):
    return vt_ref[0, j, h * HEAD_DIM:(h + 1) * HEAD_DIM, :]


def _fox_kernel(fend_ref, margin_ref, qf_ref, kf_ref, vt_ref, o_ref, m_scr, l_scr, acc_scr, s_scr):
    bi = pl.program_id(0)
    hg = pl.program_id(1)
    i = pl.program_id(2)
    t = qf_ref.shape[1]
    heads = range(HEADS_PER_STEP)

    def tile_needed(j):
        need = False
        for h in heads:
            row = (bi * pl.num_programs(1) + hg) * HEADS_PER_STEP + h
            gap = fend_ref[row, j] - fend_ref[row, jnp.maximum(i - 1, 0)]
            need = jnp.logical_or(need, gap <= margin_ref[0])
        return need

    first = lax.while_loop(
        lambda j: jnp.logical_and(j > 0, tile_needed(jnp.maximum(j - 1, 0))), lambda j: j - 1, i)

    key, qry = _key_query_iota(t)
    causal = key <= qry
    lane = lax.broadcasted_iota(jnp.int32, (t, 2 * LANES), 1)
    q = []
    for h in heads:
        qpair = qf_ref[0, :, (h // 2) * 2 * LANES:(h // 2 + 1) * 2 * LANES]
        q_lo, f_lo = (h % 2) * HEAD_DIM, LANES + (h % 2) * N_FEATURES
        own = jnp.logical_or(jnp.logical_and(lane >= q_lo, lane < q_lo + HEAD_DIM),
                             jnp.logical_and(lane >= f_lo, lane < f_lo + N_FEATURES))
        q.append(jnp.where(own, qpair, jnp.zeros_like(qpair)))
    m_scr[...] = jnp.full_like(m_scr, -jnp.inf)
    l_scr[...] = jnp.zeros_like(l_scr)
    acc_scr[...] = jnp.zeros_like(acc_scr)

    ones_rows = jnp.ones((ONES_ROWS, t), BF16)

    def logits(j):
        start = pl.multiple_of(j * t, t)
        return [_dot_nt(kf_ref[0, pl.ds(start, t), (h // 2) * 2 * LANES:(h // 2 + 1) * 2 * LANES], q[h])
                for h in heads]

    def softmax_step(s, j, diagonal):
        if diagonal:
            s = [jnp.where(causal, sh, -jnp.inf) for sh in s]
        m_old = [m_scr[h] for h in heads]
        m_new = [jnp.maximum(m_old[h], jnp.max(s[h], axis=0, keepdims=True)) for h in heads]
        alpha = [jnp.exp2(m_old[h] - m_new[h]) for h in heads]
        p = [jnp.exp2(s[h] - m_new[h]).astype(BF16) for h in heads]
        pv = [_dot(jnp.concatenate([_value_rows(vt_ref, j, h), ones_rows], axis=0), p[h])
              for h in heads]
        for h in heads:
            m_scr[h] = m_new[h]
            l_scr[h] = alpha[h] * l_scr[h] + pv[h][HEAD_DIM:HEAD_DIM + 1, :]
            acc_scr[h] = alpha[h] * acc_scr[h] + pv[h][:HEAD_DIM, :]

    def visit(j, carry):
        s_next = logits(j + 1)
        softmax_step([s_scr[h] for h in heads], j, False)
        for h in heads:
            s_scr[h] = s_next[h]
        return carry

    s_first = logits(first)
    for h in heads:
        s_scr[h] = s_first[h]
    lax.fori_loop(first, i, visit, 0)
    softmax_step([s_scr[h] for h in heads], i, True)
    _store_heads(o_ref, [acc_scr[h] * (1.0 / l_scr[h]) for h in heads])


ONES_ROWS = BF16_ROWS


def _attn_scratch(t):
    n = HEADS_PER_STEP
    return [pltpu.VMEM((n, 1, t), F32), pltpu.VMEM((n, 1, t), F32), pltpu.VMEM((n, HEAD_DIM, t), F32),
            pltpu.VMEM((n, t, t), F32)]


def _fox(fend, margin, qf, kf, fvt):
    b, nblk, width, t = fvt.shape
    s = nblk * t
    vw = PAIRS_PER_STEP * LANES
    qw = HEADS_PER_STEP * LANES
    smem = pl.BlockSpec(memory_space=pltpu.SMEM)
    return pl.pallas_call(
        _fox_kernel,
        grid=(b, width // vw, nblk),
        in_specs=[
            smem, smem,
            pl.BlockSpec((1, t, qw), lambda bi, hg, i: (bi, i, hg)),
            pl.BlockSpec((1, s, qw), lambda bi, hg, i: (bi, 0, hg)),
            pl.BlockSpec((1, nblk, vw, t), lambda bi, hg, i: (bi, 0, hg, 0)),
        ],
        out_specs=pl.BlockSpec((1, t, vw), lambda bi, hg, i: (bi, i, hg)),
        out_shape=jax.ShapeDtypeStruct((b, s, width), BF16),
        scratch_shapes=_attn_scratch(t),
        compiler_params=_params("parallel", "parallel", "arbitrary"),
        name="fox_attention",
    )(fend, margin, qf, kf, fvt)


def _sb_kernel(q_ref, k_ref, vt_ref, u_ref, o_ref, r_scr, acc_scr):
    i = pl.program_id(2)
    t = q_ref.shape[1]
    heads = range(HEADS_PER_STEP)
    key, qry = _key_query_iota(t)
    strict = key < qry
    qlane = lax.broadcasted_iota(jnp.int32, (t, LANES), 1)
    q = []
    for h in heads:
        qpair = q_ref[0, :, (h // 2) * LANES:(h // 2 + 1) * LANES]
        q.append(jnp.where((qlane < HEAD_DIM) == (h % 2 == 0), qpair, jnp.zeros_like(qpair)))
    r_scr[...] = jnp.zeros_like(r_scr)
    acc_scr[...] = jnp.zeros_like(acc_scr)

    def logits(j):
        start = pl.multiple_of(j * t, t)
        return [_dot_nt(k_ref[0, pl.ds(start, t), (h // 2) * LANES:(h // 2 + 1) * LANES], q[h])
                for h in heads]

    def stick_step(z, j, diagonal):
        sp = [jnp.maximum(zh, 0.0) + jnp.log2(1.0 + jnp.exp2(-jnp.abs(zh))) for zh in z]
        log_beta = [z[h] - sp[h] for h in heads]
        if diagonal:
            sp = [jnp.where(strict, sh, 0.0) for sh in sp]
        sums = [_dot(u_ref[...], sh.astype(BF16)) for sh in sp]
        a = [jnp.exp2(log_beta[h] - sums[h][:t, :]) for h in heads]
        if diagonal:
            a = [jnp.where(strict, ah, 0.0) for ah in a]
        pv = [_dot(_value_rows(vt_ref, j, h), a[h].astype(BF16)) for h in heads]
        for h in heads:
            r = r_scr[h]
            acc_scr[h] += jnp.exp2(-r) * pv[h]
            r_scr[h] = r + sums[h][t:t + 1, :]

    def least_r():
        r = r_scr[0]
        for h in heads[1:]:
            r = jnp.minimum(r, r_scr[h])
        return jnp.min(r)

    def body(carry):
        n, _ = carry
        stick_step(logits(i - 1 - n), i - 1 - n, False)
        return n + 1, least_r()

    stick_step(logits(i), i, True)
    lax.while_loop(lambda c: jnp.logical_and(c[0] < i, c[1] < ZERO_WEIGHT_BITS), body,
                   (jnp.int32(0), least_r()))
    _store_heads(o_ref, [acc_scr[h] for h in heads])


def _sb(sq, sk, svt, later_mat):
    b, nblk, width, t = svt.shape
    s = nblk * t
    vw = PAIRS_PER_STEP * LANES
    return pl.pallas_call(
        _sb_kernel,
        grid=(b, width // vw, nblk),
        in_specs=[
            pl.BlockSpec((1, t, vw), lambda bi, hg, i: (bi, i, hg)),
            pl.BlockSpec((1, s, vw), lambda bi, hg, i: (bi, 0, hg)),
            pl.BlockSpec((1, nblk, vw, t), lambda bi, hg, i: (bi, 0, hg, 0)),
            _const_spec(later_mat.shape),
        ],
        out_specs=pl.BlockSpec((1, t, vw), lambda bi, hg, i: (bi, i, hg)),
        out_shape=jax.ShapeDtypeStruct((b, s, width), BF16),
        scratch_shapes=_attn_scratch(t)[1:3],
        compiler_params=_params("parallel", "parallel", "arbitrary"),
        name="sb_attention",
    )(sq, sk, svt, later_mat)


N_POST_WEIGHTS = 8


def _post_kernel(stage_shapes, x_ref, yf_ref, ys_ref, gf_ref, gs_ref, p_ref, g2_ref, gp_ref, *rest):
    w_hbm, o_ref = rest[:N_POST_WEIGHTS], rest[N_POST_WEIGHTS]
    w_vmem = rest[N_POST_WEIGHTS + 1:2 * N_POST_WEIGHTS + 1]
    dma = rest[2 * N_POST_WEIGHTS + 1:]

    @pl.when(pl.program_id(0) == 0)
    def _():
        _load_weights(w_hbm, w_vmem, stage_shapes, dma[:-1], dma[-1])

    wbf_ref, wbs_ref, wo_ref, wg_ref, wu_ref, wd_ref, wpg_ref, wpp_ref = w_vmem
    merged =(gf_ref[...].astype(F32) * _dot(yf_ref[...], wbf_ref[...])
              + gs_ref[...].astype(F32) * _dot(ys_ref[...], wbs_ref[...]))
    x = x_ref[...] + _dot(merged.astype(BF16), wo_ref[...])

    h = _rms_rows(x, g2_ref[...]).astype(BF16)
    a = _dot(h, wg_ref[...])
    u = _dot(h, wu_ref[...])
    x = x + 0.5 * _dot((a * _sigmoid(a) * u).astype(BF16), wd_ref[...])

    h = _rms_rows(x, gp_ref[...]).astype(BF16)
    gate = _sigmoid(_dot(h, wpg_ref[...]))
    o_ref[...] = x + gate * _dot(p_ref[...].astype(BF16), wpp_ref[...])


def _post(x2d, yf, ys, gf, gs, p2d, gains, weights):
    n, d = x2d.shape
    tm = POST_TILE
    assert len(weights) == N_POST_WEIGHTS

    def tok(a):
        return pl.BlockSpec((tm, a.shape[1]), lambda i: (i, 0))

    toks = [x2d, yf, ys, gf, gs, p2d]
    homes, stage_shapes, dma = _weight_scratch(weights)
    return pl.pallas_call(
        functools.partial(_post_kernel, stage_shapes),
        grid=(n // tm,),
        in_specs=[tok(a) for a in toks] + [_const_spec(g.shape) for g in gains] + [_HBM] * len(weights),
        out_specs=tok(x2d),
        out_shape=jax.ShapeDtypeStruct((n, d), F32),
        scratch_shapes=homes + dma,
        compiler_params=_params("arbitrary"),
        name="post",
    )(*toks, *gains, *weights)


@functools.lru_cache(maxsize=None)
def _layout_constants(n_heads, tile):
    width = n_heads * HEAD_DIM
    sel = np.zeros((width, LANES), np.float32)
    pq = np.zeros((3 * LANES, width), np.float32)
    pk = np.zeros((3 * LANES, width), np.float32)
    cq = np.zeros((1, width), np.float32)
    ck = np.zeros((1, width), np.float32)
    for h in range(n_heads):
        sel[h * HEAD_DIM:(h + 1) * HEAD_DIM, h] = 1.0 / HEAD_DIM
        base = (h // 2) * LANES + (h % 2) * N_FEATURES
        for part in range(3):
            pq[part * LANES + h, base + part] = 1.0
            pk[part * LANES + h, base + 3 + part] = -1.0
            cq[0, base + 3 + part] = 1.0
            ck[0, base + part] = 1.0
    selt = (sel.T > 0).astype(np.float32)
    tri = np.tril(np.ones((tile, tile), np.float32))
    return sel, selt, tri, pq, pk, cq, ck


@functools.lru_cache(maxsize=None)
def _later_matrix(tile):
    return np.concatenate([np.triu(np.ones((tile, tile), np.float32), k=1),
                           np.ones((ONES_ROWS, tile), np.float32)], axis=0)


def kernel(x, p, ffn1_norm, ffn1_w_gate, ffn1_w_up, ffn1_w_down, mix_norm, w_in, forget_bias, q_norm, k_norm, w_branch_fox, w_branch_sb, w_out, ffn2_norm, ffn2_w_gate, ffn2_w_up, ffn2_w_down, ple_norm, w_ple_gate, w_ple_proj):
    b, s, d = x.shape
    depth = w_in.shape[0]
    fox_w = w_branch_fox.shape[1]
    sb_w = w_branch_sb.shape[1]
    n_heads = forget_bias.shape[1]
    assert fox_w == n_heads * HEAD_DIM and sb_w == fox_w and n_heads <= LANES
    assert s % TOKEN_TILE == 0 and TOKEN_TILE % ATTN_TILE == 0 and TOKEN_TILE // ATTN_TILE <= SUBLANES
    assert n_heads % HEADS_PER_STEP == 0 and 2 * N_FEATURES <= LANES
    assert (b * s) % FFN_TILE == 0 and (b * s) % POST_TILE == 0

    sel, selt, tri, pq, pk, cq, ck = _layout_constants(n_heads, TOKEN_TILE)
    later = jnp.asarray(_later_matrix(ATTN_TILE), BF16)

    xf = x.reshape(b * s, d)
    for i in range(depth):
        xf = _ffn(xf, ffn1_norm[i][None], ffn1_w_gate[i], ffn1_w_up[i], ffn1_w_down[i])

        w = w_in[i]
        o_f = 3 * fox_w
        o_sb = o_f + n_heads
        o_gate = o_sb + 3 * sb_w
        wf = jnp.pad(w[:, o_f:o_sb], ((0, 0), (0, LANES - n_heads)))
        w_tok = jnp.concatenate([w[:, :2 * fox_w], w[:, o_sb:o_sb + 2 * sb_w], w[:, o_gate:], wf],
                                axis=1).astype(BF16)
        w_val = jnp.concatenate([w[:, 2 * fox_w:o_f], w[:, o_sb + 2 * sb_w:o_gate]],
                                axis=1).astype(BF16)
        fbias = jnp.pad(forget_bias[i][None], ((0, 0), (0, LANES - n_heads)))
        gq = jnp.tile(q_norm[i], n_heads)[None] * (LOG2E * HEAD_DIM ** -0.5)
        gk = jnp.tile(k_norm[i], n_heads)[None]
        consts = [mix_norm[i][None], w_tok, w_val, fbias, gq, gk,
                  jnp.asarray(sel, BF16), jnp.asarray(selt, BF16), jnp.asarray(tri, BF16),
                  jnp.asarray(pq, BF16), jnp.asarray(pk, BF16), jnp.asarray(cq), jnp.asarray(ck)]
        qf, kf, fvt, sq, sk, svt, gf, gs, fend = _inproj(xf.reshape(b, s, d), consts)

        fend = fend[:, :, :TOKEN_TILE // ATTN_TILE, :n_heads].reshape(b, s // ATTN_TILE, n_heads)
        fend = fend.transpose(0, 2, 1).reshape(b * n_heads, s // ATTN_TILE)
        ub = LOG2E * HEAD_DIM ** 0.5 * jnp.max(jnp.abs(q_norm[i])) * jnp.max(jnp.abs(k_norm[i]))
        margin = (2.0 * ub + ZERO_WEIGHT_BITS).reshape(1).astype(F32)

        y_fox = _fox(fend, margin, qf, kf, fvt)
        y_sb = _sb(sq, sk, svt, later)

        weights = [w_branch_fox[i], w_branch_sb[i], w_out[i], ffn2_w_gate[i], ffn2_w_up[i],
                   ffn2_w_down[i], w_ple_gate[i], w_ple_proj[i]]
        xf = _post(xf, y_fox.reshape(b * s, fox_w), y_sb.reshape(b * s, sb_w),
                   gf.reshape(b * s, d), gs.reshape(b * s, d), p[i].reshape(b * s, -1),
                   [ffn2_norm[i][None], ple_norm[i][None]], weights)
    return xf.reshape(b, s, d)
```

```python
import functools

import jax
import jax.numpy as jnp
import numpy as np
from jax import lax
from jax.experimental import pallas as pl
from jax.experimental.pallas import tpu as pltpu

F32 = jnp.float32
BF16 = jnp.bfloat16

EPS = 1e-6
HEAD_DIM = 64
LOG2E = 1.4426950408889634
LANES = 128
SUBLANES = 8
BF16_ROWS = 2 * SUBLANES
ZERO_WEIGHT_BITS = 152.0
VMEM_LIMIT_BYTES = 56 * 1024 * 1024

TOKEN_TILE = 512
FFN_TILE = 1024
POST_TILE = 512
ATTN_TILE = 256

NT_DIMS = (((1,), (1,)), ((), ()))


def _dot(a, b):
    return jnp.dot(a, b, preferred_element_type=F32)


def _dot_nt(a, b):
    return lax.dot_general(a, b, NT_DIMS, preferred_element_type=F32)


def _split2(a):
    hi = a.astype(BF16)
    lo = (a - hi.astype(F32)).astype(BF16)
    return hi, lo


def _split3(a):
    p1 = a.astype(BF16)
    r1 = a - p1.astype(F32)
    p2 = r1.astype(BF16)
    r2 = r1 - p2.astype(F32)
    return p1, p2, r2.astype(BF16)


def _rms_rows(x, g):
    ms = jnp.mean(x * x, axis=-1, keepdims=True)
    return x * lax.rsqrt(ms + EPS) * g


def _sigmoid(x):
    return 1.0 / (1.0 + jnp.exp(-x))


def _params(*sem):
    return pltpu.CompilerParams(dimension_semantics=sem, vmem_limit_bytes=VMEM_LIMIT_BYTES)


def _const_spec(shape):
    nd = len(shape)
    return pl.BlockSpec(shape, lambda *_: (0,) * nd, pipeline_mode=pl.Buffered(1))


STAGE_BYTES = 1024 * 1024
STAGE_SLOTS = 4


def _stage_rows(w):
    rows, cols = w.shape
    ch = max(BF16_ROWS, STAGE_BYTES // (4 * cols) // BF16_ROWS * BF16_ROWS)
    while rows % ch:
        ch -= BF16_ROWS
    return ch


def _weight_scratch(weights):
    homes = [pltpu.VMEM(w.shape, BF16) for w in weights]
    stage_shapes = sorted({(_stage_rows(w), w.shape[1]) for w in weights})
    stages = [pltpu.VMEM((STAGE_SLOTS,) + s, F32) for s in stage_shapes]
    return homes, stage_shapes, stages + [pltpu.SemaphoreType.DMA((len(stage_shapes), STAGE_SLOTS))]


def _load_weights(w_hbm, w_vmem, stage_shapes, stages, sem):
    jobs, used = [], [0] * len(stage_shapes)
    for src, dst in zip(w_hbm, w_vmem):
        ch = _stage_rows(src)
        k = stage_shapes.index((ch, src.shape[1]))
        for c in range(src.shape[0] // ch):
            slot = used[k] % STAGE_SLOTS
            used[k] += 1
            copy = pltpu.make_async_copy(src.at[pl.ds(c * ch, ch), :], stages[k].at[slot], sem.at[k, slot])
            jobs.append((copy, dst, c * ch, ch, stages[k], slot))
    ahead = STAGE_SLOTS - 1
    for copy, *_ in jobs[:ahead]:
        copy.start()
    for n, (copy, dst, row, ch, stage, slot) in enumerate(jobs):
        if n + ahead < len(jobs):
            jobs[n + ahead][0].start()
        copy.wait()
        dst[pl.ds(row, ch), :] = stage[slot].astype(BF16)


_HBM = pl.BlockSpec(memory_space=pl.ANY)


def _ffn_kernel(stage_shapes, x_ref, g_ref, wg_hbm, wu_hbm, wd_hbm, o_ref, wg_ref, wu_ref, wd_ref, *dma):
    @pl.when(pl.program_id(0) == 0)
    def _():
        _load_weights([wg_hbm, wu_hbm, wd_hbm], [wg_ref, wu_ref, wd_ref], stage_shapes, dma[:-1], dma[-1])

    x = x_ref[...]
    h = _rms_rows(x, g_ref[...]).astype(BF16)
    a = _dot(h, wg_ref[...])
    u = _dot(h, wu_ref[...])
    act = (a * _sigmoid(a) * u).astype(BF16)
    o_ref[...] = x + 0.5 * _dot(act, wd_ref[...])


def _ffn(x2d, g, wg, wu, wd):
    n, d = x2d.shape
    tm = FFN_TILE
    homes, stage_shapes, dma = _weight_scratch([wg, wu, wd])
    return pl.pallas_call(
        functools.partial(_ffn_kernel, stage_shapes),
        grid=(n // tm,),
        in_specs=[pl.BlockSpec((tm, d), lambda i: (i, 0)), _const_spec(g.shape), _HBM, _HBM, _HBM],
        out_specs=pl.BlockSpec((tm, d), lambda i: (i, 0)),
        out_shape=jax.ShapeDtypeStruct((n, d), F32),
        scratch_shapes=homes + dma,
        compiler_params=_params("arbitrary"),
        name="ffn",
    )(x2d, g, wg, wu, wd)


def _inproj_kernel(x_ref, g_ref, wtok_ref, wvalt_ref,
                   fbias_ref, gq_ref, gk_ref, sel_ref, selt_ref, tri_ref,
                   pq_ref, pk_ref, cq_ref, ck_ref,
                   qf_ref, kf_ref, fvt_ref, sq_ref, sk_ref, svt_ref, gf_ref, gs_ref, fend_ref,
                   carry_scr):
    @pl.when(pl.program_id(1) == 0)
    def _():
        carry_scr[...] = jnp.zeros_like(carry_scr)

    h = _rms_rows(x_ref[0], g_ref[...]).astype(BF16)
    t = fvt_ref.shape[3]
    n_heads = qf_ref.shape[2] // LANES
    width = sq_ref.shape[2]
    d = gf_ref.shape[2]
    c_k, c_sb, c_gate, c_f = width, 2 * width, 4 * width, 4 * width + 2 * d

    yf = _dot_nt(h, wtok_ref[c_f:, :]) + fbias_ref[...]
    yg = _dot_nt(h, wtok_ref[c_gate:c_f, :])

    lf = jnp.minimum(yf, 0.0) - jnp.log1p(jnp.exp(-jnp.abs(yf)))
    lane = lax.broadcasted_iota(jnp.int32, lf.shape, 1)
    lf = jnp.where(lane < n_heads, lf, 0.0)
    c = _dot(tri_ref[...], jnp.concatenate(_split3(lf), axis=1))

    yq = _dot_nt(h, wtok_ref[:c_k, :])
    yk = _dot_nt(h, wtok_ref[c_k:c_sb, :])
    msq = _dot((yq * yq).astype(BF16), sel_ref[...])
    msk = _dot((yk * yk).astype(BF16), sel_ref[...])

    ysb = _dot_nt(h, wtok_ref[c_sb:c_gate, :])

    def spread_rsqrt(ms):
        rhi, rlo = _split2(lax.rsqrt(ms + EPS))
        return _dot(rhi, selt_ref[...]) + _dot(rlo, selt_ref[...])

    rq = spread_rsqrt(msq)
    rk = spread_rsqrt(msk)

    fvt = _dot_nt(wvalt_ref[:width, :], h).astype(BF16)
    svt = _dot_nt(wvalt_ref[width:2 * width, :], h).astype(BF16)

    f_cum = c[:, :LANES] + c[:, LANES:2 * LANES] + c[:, 2 * LANES:] + carry_scr[...]
    carry_scr[...] = f_cum[-1:, :]
    f_bits = f_cum * LOG2E
    fparts = jnp.concatenate(_split3(f_bits), axis=1)
    featq = (_dot(fparts, pq_ref[...]) + cq_ref[...]).astype(BF16)
    featk = (_dot(fparts, pk_ref[...]) + ck_ref[...]).astype(BF16)

    yg = _sigmoid(yg)
    gf_ref[0] = yg[:, :d].astype(BF16)
    gs_ref[0] = yg[:, d:].astype(BF16)

    sq_ref[0] = (ysb[:, :width] * (LOG2E * HEAD_DIM ** -0.5)).astype(BF16)
    sk_ref[0] = ysb[:, width:].astype(BF16)

    for c in range(fvt_ref.shape[1]):
        fvt_ref[0, c] = fvt[:, c * t:(c + 1) * t]
        svt_ref[0, c] = svt[:, c * t:(c + 1) * t]

    fend_ref[0, 0] = jnp.zeros(fend_ref.shape[2:], F32)
    for c in range(fvt_ref.shape[1]):
        fend_ref[0, 0, c:c + 1, :] = f_bits[(c + 1) * t - 1:(c + 1) * t, :]

    q = (yq * rq * gq_ref[...]).astype(BF16)
    k = (yk * rk * gk_ref[...]).astype(BF16)
    for pr in range(q.shape[1] // LANES):
        src = slice(pr * LANES, (pr + 1) * LANES)
        qf_ref[0, :, 2 * pr * LANES:(2 * pr + 1) * LANES] = q[:, src]
        qf_ref[0, :, (2 * pr + 1) * LANES:(2 * pr + 2) * LANES] = featq[:, src]
        kf_ref[0, :, 2 * pr * LANES:(2 * pr + 1) * LANES] = k[:, src]
        kf_ref[0, :, (2 * pr + 1) * LANES:(2 * pr + 2) * LANES] = featk[:, src]


def _inproj(x3d, consts):
    b, s, d = x3d.shape
    tm = TOKEN_TILE
    t = ATTN_TILE
    width = consts[2].shape[0] // 2
    spread = 2 * width
    ins = [x3d] + list(consts)
    in_specs = [pl.BlockSpec((1, tm, d), lambda bi, i: (bi, i, 0))]
    in_specs += [_const_spec(c.shape) for c in consts]

    def tok(n):
        return pl.BlockSpec((1, tm, n), lambda bi, i: (bi, i, 0))

    def shp(n):
        return jax.ShapeDtypeStruct((b, s, n), BF16)

    vt_spec = pl.BlockSpec((1, tm // t, width, t), lambda bi, i: (bi, i, 0, 0))
    vt_shape = jax.ShapeDtypeStruct((b, s // t, width, t), BF16)
    fend_spec = pl.BlockSpec((1, 1, SUBLANES, LANES), lambda bi, i: (bi, i, 0, 0))
    fend_shape = jax.ShapeDtypeStruct((b, s // tm, SUBLANES, LANES), F32)

    return pl.pallas_call(
        _inproj_kernel,
        grid=(b, s // tm),
        in_specs=in_specs,
        out_specs=[tok(spread), tok(spread), vt_spec, tok(width), tok(width), vt_spec,
                   tok(d), tok(d), fend_spec],
        out_shape=[shp(spread), shp(spread), vt_shape, shp(width), shp(width), vt_shape,
                   shp(d), shp(d), fend_shape],
        scratch_shapes=[pltpu.VMEM((1, LANES), F32)],
        compiler_params=_params("parallel", "arbitrary"),
        name="inproj",
    )(*ins)


HEADS_PER_STEP = 8
PAIRS_PER_STEP = HEADS_PER_STEP // 2
N_FEATURES = 6


def _key_query_iota(t):
    return (lax.broadcasted_iota(jnp.int32, (t, t), 0), lax.broadcasted_iota(jnp.int32, (t, t), 1))


def _store_heads(o_ref, outs):
    for pr in range(PAIRS_PER_STEP):
        pair = jnp.concatenate([outs[2 * pr], outs[2 * pr + 1]], axis=0)
        o_ref[0, :, pr * LANES:(pr + 1) * LANES] = pair.T.astype(o_ref.dtype)


def _value_rows(vt_ref, j, h):
    return vt_ref[0, j, h * HEAD_DIM:(h + 1) * HEAD_DIM, :]


def _fox_kernel(fend_ref, margin_ref, qf_ref, kf_ref, vt_ref, o_ref, m_scr, l_scr, acc_scr, s_scr):
    bi = pl.program_id(0)
    hg = pl.program_id(1)
    i = pl.program_id(2)
    t = qf_ref.shape[1]
    heads = range(HEADS_PER_STEP)

    def tile_needed(j):
        need = False
        for h in heads:
            row = (bi * pl.num_programs(1) + hg) * HEADS_PER_STEP + h
            gap = fend_ref[row, j] - fend_ref[row, jnp.maximum(i - 1, 0)]
            need = jnp.logical_or(need, gap <= margin_ref[0])
        return need

    first = lax.while_loop(
        lambda j: jnp.logical_and(j > 0, tile_needed(jnp.maximum(j - 1, 0))), lambda j: j - 1, i)

    key, qry = _key_query_iota(t)
    causal = key <= qry
    lane = lax.broadcasted_iota(jnp.int32, (t, 2 * LANES), 1)
    q = []
    for h in heads:
        qpair = qf_ref[0, :, (h // 2) * 2 * LANES:(h // 2 + 1) * 2 * LANES]
        q_lo, f_lo = (h % 2) * HEAD_DIM, LANES + (h % 2) * N_FEATURES
        own = jnp.logical_or(jnp.logical_and(lane >= q_lo, lane < q_lo + HEAD_DIM),
                             jnp.logical_and(lane >= f_lo, lane < f_lo + N_FEATURES))
        q.append(jnp.where(own, qpair, jnp.zeros_like(qpair)))
    m_scr[...] = jnp.full_like(m_scr, -jnp.inf)
    l_scr[...] = jnp.zeros_like(l_scr)
    acc_scr[...] = jnp.zeros_like(acc_scr)

    ones_rows = jnp.ones((ONES_ROWS, t), BF16)

    def logits(j):
        start = pl.multiple_of(j * t, t)
        return [_dot_nt(kf_ref[0, pl.ds(start, t), (h // 2) * 2 * LANES:(h // 2 + 1) * 2 * LANES], q[h])
                for h in heads]

    def softmax_step(s, j, diagonal):
        if diagonal:
            s = [jnp.where(causal, sh, -jnp.inf) for sh in s]
        m_old = [m_scr[h] for h in heads]
        m_new = [jnp.maximum(m_old[h], jnp.max(s[h], axis=0, keepdims=True)) for h in heads]
        alpha = [jnp.exp2(m_old[h] - m_new[h]) for h in heads]
        p = [jnp.exp2(s[h] - m_new[h]).astype(BF16) for h in heads]
        pv = [_dot(jnp.concatenate([_value_rows(vt_ref, j, h), ones_rows], axis=0), p[h])
              for h in heads]
        for h in heads:
            m_scr[h] = m_new[h]
            l_scr[h] = alpha[h] * l_scr[h] + pv[h][HEAD_DIM:HEAD_DIM + 1, :]
            acc_scr[h] = alpha[h] * acc_scr[h] + pv[h][:HEAD_DIM, :]

    def visit(j, carry):
        s_next = logits(j + 1)
        softmax_step([s_scr[h] for h in heads], j, False)
        for h in heads:
            s_scr[h] = s_next[h]
        return carry

    s_first = logits(first)
    for h in heads:
        s_scr[h] = s_first[h]
    lax.fori_loop(first, i, visit, 0)
    softmax_step([s_scr[h] for h in heads], i, True)
    _store_heads(o_ref, [acc_scr[h] * (1.0 / l_scr[h]) for h in heads])


ONES_ROWS = BF16_ROWS


def _attn_scratch(t):
    n = HEADS_PER_STEP
    return [pltpu.VMEM((n, 1, t), F32), pltpu.VMEM((n, 1, t), F32), pltpu.VMEM((n, HEAD_DIM, t), F32),
            pltpu.VMEM((n, t, t), F32)]


def _fox(fend, margin, qf, kf, fvt):
    b, nblk, width, t = fvt.shape
    s = nblk * t
    vw = PAIRS_PER_STEP * LANES
    qw = HEADS_PER_STEP * LANES
    smem = pl.BlockSpec(memory_space=pltpu.SMEM)
    return pl.pallas_call(
        _fox_kernel,
        grid=(b, width // vw, nblk),
        in_specs=[
            smem, smem,
            pl.BlockSpec((1, t, qw), lambda bi, hg, i: (bi, i, hg)),
            pl.BlockSpec((1, s, qw), lambda bi, hg, i: (bi, 0, hg)),
            pl.BlockSpec((1, nblk, vw, t), lambda bi, hg, i: (bi, 0, hg, 0)),
        ],
        out_specs=pl.BlockSpec((1, t, vw), lambda bi, hg, i: (bi, i, hg)),
        out_shape=jax.ShapeDtypeStruct((b, s, width), BF16),
        scratch_shapes=_attn_scratch(t),
        compiler_params=_params("parallel", "parallel", "arbitrary"),
        name="fox_attention",
    )(fend, margin, qf, kf, fvt)


def _sb_kernel(q_ref, k_ref, vt_ref, u_ref, o_ref, r_scr, acc_scr):
    i = pl.program_id(2)
    t = q_ref.shape[1]
    heads = range(HEADS_PER_STEP)
    key, qry = _key_query_iota(t)
    strict = key < qry
    qlane = lax.broadcasted_iota(jnp.int32, (t, LANES), 1)
    q = []
    for h in heads:
        qpair = q_ref[0, :, (h // 2) * LANES:(h // 2 + 1) * LANES]
        q.append(jnp.where((qlane < HEAD_DIM) == (h % 2 == 0), qpair, jnp.zeros_like(qpair)))
    r_scr[...] = jnp.zeros_like(r_scr)
    acc_scr[...] = jnp.zeros_like(acc_scr)

    def logits(j):
        start = pl.multiple_of(j * t, t)
        return [_dot_nt(k_ref[0, pl.ds(start, t), (h // 2) * LANES:(h // 2 + 1) * LANES], q[h])
                for h in heads]

    def stick_step(z, j, diagonal):
        sp = [jnp.maximum(zh, 0.0) + jnp.log2(1.0 + jnp.exp2(-jnp.abs(zh))) for zh in z]
        log_beta = [z[h] - sp[h] for h in heads]
        if diagonal:
            sp = [jnp.where(strict, sh, 0.0) for sh in sp]
        sums = [_dot(u_ref[...], sh.astype(BF16)) for sh in sp]
        a = [jnp.exp2(log_beta[h] - sums[h][:t, :]) for h in heads]
        if diagonal:
            a = [jnp.where(strict, ah, 0.0) for ah in a]
        pv = [_dot(_value_rows(vt_ref, j, h), a[h].astype(BF16)) for h in heads]
        for h in heads:
            r = r_scr[h]
            acc_scr[h] += jnp.exp2(-r) * pv[h]
            r_scr[h] = r + sums[h][t:t + 1, :]

    def least_r():
        r = r_scr[0]
        for h in heads[1:]:
            r = jnp.minimum(r, r_scr[h])
        return jnp.min(r)

    def body(carry):
        n, _ = carry
        stick_step(logits(i - 1 - n), i - 1 - n, False)
        return n + 1, least_r()

    stick_step(logits(i), i, True)
    lax.while_loop(lambda c: jnp.logical_and(c[0] < i, c[1] < ZERO_WEIGHT_BITS), body,
                   (jnp.int32(0), least_r()))
    _store_heads(o_ref, [acc_scr[h] for h in heads])


def _sb(sq, sk, svt, later_mat):
    b, nblk, width, t = svt.shape
    s = nblk * t
    vw = PAIRS_PER_STEP * LANES
    return pl.pallas_call(
        _sb_kernel,
        grid=(b, width // vw, nblk),
        in_specs=[
            pl.BlockSpec((1, t, vw), lambda bi, hg, i: (bi, i, hg)),
            pl.BlockSpec((1, s, vw), lambda bi, hg, i: (bi, 0, hg)),
            pl.BlockSpec((1, nblk, vw, t), lambda bi, hg, i: (bi, 0, hg, 0)),
            _const_spec(later_mat.shape),
        ],
        out_specs=pl.BlockSpec((1, t, vw), lambda bi, hg, i: (bi, i, hg)),
        out_shape=jax.ShapeDtypeStruct((b, s, width), BF16),
        scratch_shapes=_attn_scratch(t)[1:3],
        compiler_params=_params("parallel", "parallel", "arbitrary"),
        name="sb_attention",
    )(sq, sk, svt, later_mat)


N_POST_WEIGHTS = 8


def _post_kernel(stage_shapes, x_ref, yf_ref, ys_ref, gf_ref, gs_ref, p_ref, g2_ref, gp_ref, *rest):
    w_hbm, o_ref = rest[:N_POST_WEIGHTS], rest[N_POST_WEIGHTS]
    w_vmem = rest[N_POST_WEIGHTS + 1:2 * N_POST_WEIGHTS + 1]
    dma = rest[2 * N_POST_WEIGHTS + 1:]

    @pl.when(pl.program_id(0) == 0)
    def _():
        _load_weights(w_hbm, w_vmem, stage_shapes, dma[:-1], dma[-1])

    wbf_ref, wbs_ref, wo_ref, wg_ref, wu_ref, wd_ref, wpg_ref, wpp_ref = w_vmem
    merged =(gf_ref[...].astype(F32) * _dot(yf_ref[...], wbf_ref[...])
              + gs_ref[...].astype(F32) * _dot(ys_ref[...], wbs_ref[...]))
    x = x_ref[...] + _dot(merged.astype(BF16), wo_ref[...])

    h = _rms_rows(x, g2_ref[...]).astype(BF16)
    a = _dot(h, wg_ref[...])
    u = _dot(h, wu_ref[...])
    x = x + 0.5 * _dot((a * _sigmoid(a) * u).astype(BF16), wd_ref[...])

    h = _rms_rows(x, gp_ref[...]).astype(BF16)
    gate = _sigmoid(_dot(h, wpg_ref[...]))
    o_ref[...] = x + gate * _dot(p_ref[...].astype(BF16), wpp_ref[...])


def _post(x2d, yf, ys, gf, gs, p2d, gains, weights):
    n, d = x2d.shape
    tm = POST_TILE
    assert len(weights) == N_POST_WEIGHTS

    def tok(a):
        return pl.BlockSpec((tm, a.shape[1]), lambda i: (i, 0))

    toks = [x2d, yf, ys, gf, gs, p2d]
    homes, stage_shapes, dma = _weight_scratch(weights)
    return pl.pallas_call(
        functools.partial(_post_kernel, stage_shapes),
        grid=(n // tm,),
        in_specs=[tok(a) for a in toks] + [_const_spec(g.shape) for g in gains] + [_HBM] * len(weights),
        out_specs=tok(x2d),
        out_shape=jax.ShapeDtypeStruct((n, d), F32),
        scratch_shapes=homes + dma,
        compiler_params=_params("arbitrary"),
        name="post",
    )(*toks, *gains, *weights)


@functools.lru_cache(maxsize=None)
def _layout_constants(n_heads, tile):
    width = n_heads * HEAD_DIM
    sel = np.zeros((width, LANES), np.float32)
    pq = np.zeros((3 * LANES, width), np.float32)
    pk = np.zeros((3 * LANES, width), np.float32)
    cq = np.zeros((1, width), np.float32)
    ck = np.zeros((1, width), np.float32)
    for h in range(n_heads):
        sel[h * HEAD_DIM:(h + 1) * HEAD_DIM, h] = 1.0 / HEAD_DIM
        base = (h // 2) * LANES + (h % 2) * N_FEATURES
        for part in range(3):
            pq[part * LANES + h, base + part] = 1.0
            pk[part * LANES + h, base + 3 + part] = -1.0
            cq[0, base + 3 + part] = 1.0
            ck[0, base + part] = 1.0
    selt = (sel.T > 0).astype(np.float32)
    tri = np.tril(np.ones((tile, tile), np.float32))
    return sel, selt, tri, pq, pk, cq, ck


@functools.lru_cache(maxsize=None)
def _later_matrix(tile):
    return np.concatenate([np.triu(np.ones((tile, tile), np.float32), k=1),
                           np.ones((ONES_ROWS, tile), np.float32)], axis=0)


def kernel(x, p, ffn1_norm, ffn1_w_gate, ffn1_w_up, ffn1_w_down, mix_norm, w_in, forget_bias, q_norm, k_norm, w_branch_fox, w_branch_sb, w_out, ffn2_norm, ffn2_w_gate, ffn2_w_up, ffn2_w_down, ple_norm, w_ple_gate, w_ple_proj):
    b, s, d = x.shape
    depth = w_in.shape[0]
    fox_w = w_branch_fox.shape[1]
    sb_w = w_branch_sb.shape[1]
    n_heads = forget_bias.shape[1]
    assert fox_w == n_heads * HEAD_DIM and sb_w == fox_w and n_heads <= LANES
    assert s % TOKEN_TILE == 0 and TOKEN_TILE % ATTN_TILE == 0 and TOKEN_TILE // ATTN_TILE <= SUBLANES
    assert n_heads % HEADS_PER_STEP == 0 and 2 * N_FEATURES <= LANES
    assert (b * s) % FFN_TILE == 0 and (b * s) % POST_TILE == 0

    sel, selt, tri, pq, pk, cq, ck = _layout_constants(n_heads, TOKEN_TILE)
    later = jnp.asarray(_later_matrix(ATTN_TILE), BF16)

    xf = x.reshape(b * s, d)
    for i in range(depth):
        xf = _ffn(xf, ffn1_norm[i][None], ffn1_w_gate[i], ffn1_w_up[i], ffn1_w_down[i])

        w = jnp.swapaxes(w_in[i], 0, 1)
        o_f = 3 * fox_w
        o_sb = o_f + n_heads
        o_gate = o_sb + 3 * sb_w
        wf = jnp.pad(w[o_f:o_sb], ((0, LANES - n_heads), (0, 0)))
        w_tok = jnp.concatenate([w[:2 * fox_w], w[o_sb:o_sb + 2 * sb_w], w[o_gate:], wf],
                                axis=0).astype(BF16)
        w_val = jnp.concatenate([w[2 * fox_w:o_f], w[o_sb + 2 * sb_w:o_gate]], axis=0).astype(BF16)
        fbias = jnp.pad(forget_bias[i][None], ((0, 0), (0, LANES - n_heads)))
        gq = jnp.tile(q_norm[i], n_heads)[None] * (LOG2E * HEAD_DIM ** -0.5)
        gk = jnp.tile(k_norm[i], n_heads)[None]
        consts = [mix_norm[i][None], w_tok, w_val, fbias, gq, gk,
                  jnp.asarray(sel, BF16), jnp.asarray(selt, BF16), jnp.asarray(tri, BF16),
                  jnp.asarray(pq, BF16), jnp.asarray(pk, BF16), jnp.asarray(cq), jnp.asarray(ck)]
        qf, kf, fvt, sq, sk, svt, gf, gs, fend = _inproj(xf.reshape(b, s, d), consts)

        fend = fend[:, :, :TOKEN_TILE // ATTN_TILE, :n_heads].reshape(b, s // ATTN_TILE, n_heads)
        fend = fend.transpose(0, 2, 1).reshape(b * n_heads, s // ATTN_TILE)
        ub = LOG2E * HEAD_DIM ** 0.5 * jnp.max(jnp.abs(q_norm[i])) * jnp.max(jnp.abs(k_norm[i]))
        margin = (2.0 * ub + ZERO_WEIGHT_BITS).reshape(1).astype(F32)

        y_fox = _fox(fend, margin, qf, kf, fvt)
        y_sb = _sb(sq, sk, svt, later)

        weights = [w_branch_fox[i], w_branch_sb[i], w_out[i], ffn2_w_gate[i], ffn2_w_up[i],
                   ffn2_w_down[i], w_ple_gate[i], w_ple_proj[i]]
        xf = _post(xf, y_fox.reshape(b * s, fox_w), y_sb.reshape(b * s, sb_w),
                   gf.reshape(b * s, d), gs.reshape(b * s, d), p[i].reshape(b * s, -1),
                   [ffn2_norm[i][None], ple_norm[i][None]], weights)
    return xf.reshape(b, s, d)
```

```python
import functools

import jax
import jax.numpy as jnp
import numpy as np
from jax import lax
from jax.experimental import pallas as pl
from jax.experimental.pallas import tpu as pltpu

F32 = jnp.float32
BF16 = jnp.bfloat16

EPS = 1e-6
HEAD_DIM = 64
LOG2E = 1.4426950408889634
LANES = 128
SUBLANES = 8
BF16_ROWS = 2 * SUBLANES
ZERO_WEIGHT_BITS = 152.0
VMEM_LIMIT_BYTES = 56 * 1024 * 1024

TOKEN_TILE = 512
FFN_TILE = 1024
POST_TILE = 512
ATTN_TILE = 256

NT_DIMS = (((1,), (1,)), ((), ()))


def _dot(a, b):
    return jnp.dot(a, b, preferred_element_type=F32)


def _dot_nt(a, b):
    return lax.dot_general(a, b, NT_DIMS, preferred_element_type=F32)


def _split2(a):
    hi = a.astype(BF16)
    lo = (a - hi.astype(F32)).astype(BF16)
    return hi, lo


def _split3(a):
    p1 = a.astype(BF16)
    r1 = a - p1.astype(F32)
    p2 = r1.astype(BF16)
    r2 = r1 - p2.astype(F32)
    return p1, p2, r2.astype(BF16)


def _rms_rows(x, g):
    ms = jnp.mean(x * x, axis=-1, keepdims=True)
    return x * lax.rsqrt(ms + EPS) * g


def _sigmoid(x):
    return 1.0 / (1.0 + jnp.exp(-x))


def _params(*sem):
    return pltpu.CompilerParams(dimension_semantics=sem, vmem_limit_bytes=VMEM_LIMIT_BYTES)


def _const_spec(shape):
    nd = len(shape)
    return pl.BlockSpec(shape, lambda *_: (0,) * nd, pipeline_mode=pl.Buffered(1))


STAGE_BYTES = 1024 * 1024
STAGE_SLOTS = 4


def _stage_rows(w):
    rows, cols = w.shape
    ch = max(BF16_ROWS, STAGE_BYTES // (4 * cols) // BF16_ROWS * BF16_ROWS)
    while rows % ch:
        ch -= BF16_ROWS
    return ch


def _weight_scratch(weights):
    homes = [pltpu.VMEM(w.shape, BF16) for w in weights]
    stage_shapes = sorted({(_stage_rows(w), w.shape[1]) for w in weights})
    stages = [pltpu.VMEM((STAGE_SLOTS,) + s, F32) for s in stage_shapes]
    return homes, stage_shapes, stages + [pltpu.SemaphoreType.DMA((len(stage_shapes), STAGE_SLOTS))]


def _load_weights(w_hbm, w_vmem, stage_shapes, stages, sem):
    jobs, used = [], [0] * len(stage_shapes)
    for src, dst in zip(w_hbm, w_vmem):
        ch = _stage_rows(src)
        k = stage_shapes.index((ch, src.shape[1]))
        for c in range(src.shape[0] // ch):
            slot = used[k] % STAGE_SLOTS
            used[k] += 1
            copy = pltpu.make_async_copy(src.at[pl.ds(c * ch, ch), :], stages[k].at[slot], sem.at[k, slot])
            jobs.append((copy, dst, c * ch, ch, stages[k], slot))
    ahead = STAGE_SLOTS - 1
    for copy, *_ in jobs[:ahead]:
        copy.start()
    for n, (copy, dst, row, ch, stage, slot) in enumerate(jobs):
        if n + ahead < len(jobs):
            jobs[n + ahead][0].start()
        copy.wait()
        dst[pl.ds(row, ch), :] = stage[slot].astype(BF16)


_HBM = pl.BlockSpec(memory_space=pl.ANY)


def _ffn_kernel(stage_shapes, x_ref, g_ref, wg_hbm, wu_hbm, wd_hbm, o_ref, wg_ref, wu_ref, wd_ref, *dma):
    @pl.when(pl.program_id(0) == 0)
    def _():
        _load_weights([wg_hbm, wu_hbm, wd_hbm], [wg_ref, wu_ref, wd_ref], stage_shapes, dma[:-1], dma[-1])

    x = x_ref[...]
    h = _rms_rows(x, g_ref[...]).astype(BF16)
    a = _dot(h, wg_ref[...])
    u = _dot(h, wu_ref[...])
    act = (a * _sigmoid(a) * u).astype(BF16)
    o_ref[...] = x + 0.5 * _dot(act, wd_ref[...])


def _ffn(x2d, g, wg, wu, wd):
    n, d = x2d.shape
    tm = FFN_TILE
    homes, stage_shapes, dma = _weight_scratch([wg, wu, wd])
    return pl.pallas_call(
        functools.partial(_ffn_kernel, stage_shapes),
        grid=(n // tm,),
        in_specs=[pl.BlockSpec((tm, d), lambda i: (i, 0)), _const_spec(g.shape), _HBM, _HBM, _HBM],
        out_specs=pl.BlockSpec((tm, d), lambda i: (i, 0)),
        out_shape=jax.ShapeDtypeStruct((n, d), F32),
        scratch_shapes=homes + dma,
        compiler_params=_params("arbitrary"),
        name="ffn",
    )(x2d, g, wg, wu, wd)


def _inproj_kernel(x_ref, g_ref, wtok_ref, wvalt_ref,
                   fbias_ref, gq_ref, gk_ref, sel_ref, selt_ref, tri_ref,
                   pq_ref, pk_ref, cq_ref, ck_ref,
                   qf_ref, kf_ref, fvt_ref, sq_ref, sk_ref, svt_ref, gf_ref, gs_ref, fend_ref,
                   carry_scr):
    @pl.when(pl.program_id(1) == 0)
    def _():
        carry_scr[...] = jnp.zeros_like(carry_scr)

    h = _rms_rows(x_ref[0], g_ref[...]).astype(BF16)
    t = fvt_ref.shape[3]
    n_heads = qf_ref.shape[2] // LANES
    width = sq_ref.shape[2]
    d = gf_ref.shape[2]
    c_k, c_sb, c_gate, c_f = width, 2 * width, 4 * width, 4 * width + 2 * d

    yf = _dot_nt(h, wtok_ref[c_f:, :]) + fbias_ref[...]
    yg = _dot_nt(h, wtok_ref[c_gate:c_f, :])

    lf = jnp.minimum(yf, 0.0) - jnp.log1p(jnp.exp(-jnp.abs(yf)))
    lane = lax.broadcasted_iota(jnp.int32, lf.shape, 1)
    lf = jnp.where(lane < n_heads, lf, 0.0)
    c = _dot(tri_ref[...], jnp.concatenate(_split3(lf), axis=1))

    yq = _dot_nt(h, wtok_ref[:c_k, :])
    yk = _dot_nt(h, wtok_ref[c_k:c_sb, :])
    msq = _dot((yq * yq).astype(BF16), sel_ref[...])
    msk = _dot((yk * yk).astype(BF16), sel_ref[...])

    ysb = _dot_nt(h, wtok_ref[c_sb:c_gate, :])

    def spread_rsqrt(ms):
        rhi, rlo = _split2(lax.rsqrt(ms + EPS))
        return _dot(rhi, selt_ref[...]) + _dot(rlo, selt_ref[...])

    rq = spread_rsqrt(msq)
    rk = spread_rsqrt(msk)

    fvt = _dot_nt(wvalt_ref[:width, :], h).astype(BF16)
    svt = _dot_nt(wvalt_ref[width:2 * width, :], h).astype(BF16)

    f_cum = c[:, :LANES] + c[:, LANES:2 * LANES] + c[:, 2 * LANES:] + carry_scr[...]
    carry_scr[...] = f_cum[-1:, :]
    f_bits = f_cum * LOG2E
    fparts = jnp.concatenate(_split3(f_bits), axis=1)
    featq = (_dot(fparts, pq_ref[...]) + cq_ref[...]).astype(BF16)
    featk = (_dot(fparts, pk_ref[...]) + ck_ref[...]).astype(BF16)

    yg = _sigmoid(yg)
    gf_ref[0] = yg[:, :d].astype(BF16)
    gs_ref[0] = yg[:, d:].astype(BF16)

    sq_ref[0] = (ysb[:, :width] * (LOG2E * HEAD_DIM ** -0.5)).astype(BF16)
    sk_ref[0] = ysb[:, width:].astype(BF16)

    for c in range(fvt_ref.shape[1]):
        fvt_ref[0, c] = fvt[:, c * t:(c + 1) * t]
        svt_ref[0, c] = svt[:, c * t:(c + 1) * t]

    fend_ref[0, 0] = jnp.zeros(fend_ref.shape[2:], F32)
    for c in range(fvt_ref.shape[1]):
        fend_ref[0, 0, c:c + 1, :] = f_bits[(c + 1) * t - 1:(c + 1) * t, :]

    q = (yq * rq * gq_ref[...]).astype(BF16)
    k = (yk * rk * gk_ref[...]).astype(BF16)
    for pr in range(q.shape[1] // LANES):
        src = slice(pr * LANES, (pr + 1) * LANES)
        qf_ref[0, :, 2 * pr * LANES:(2 * pr + 1) * LANES] = q[:, src]
        qf_ref[0, :, (2 * pr + 1) * LANES:(2 * pr + 2) * LANES] = featq[:, src]
        kf_ref[0, :, 2 * pr * LANES:(2 * pr + 1) * LANES] = k[:, src]
        kf_ref[0, :, (2 * pr + 1) * LANES:(2 * pr + 2) * LANES] = featk[:, src]


def _inproj(x3d, consts):
    b, s, d = x3d.shape
    tm = TOKEN_TILE
    t = ATTN_TILE
    width = consts[2].shape[0] // 2
    spread = 2 * width
    ins = [x3d] + list(consts)
    in_specs = [pl.BlockSpec((1, tm, d), lambda bi, i: (bi, i, 0))]
    in_specs += [_const_spec(c.shape) for c in consts]

    def tok(n):
        return pl.BlockSpec((1, tm, n), lambda bi, i: (bi, i, 0))

    def shp(n):
        return jax.ShapeDtypeStruct((b, s, n), BF16)

    vt_spec = pl.BlockSpec((1, tm // t, width, t), lambda bi, i: (bi, i, 0, 0))
    vt_shape = jax.ShapeDtypeStruct((b, s // t, width, t), BF16)
    fend_spec = pl.BlockSpec((1, 1, SUBLANES, LANES), lambda bi, i: (bi, i, 0, 0))
    fend_shape = jax.ShapeDtypeStruct((b, s // tm, SUBLANES, LANES), F32)

    return pl.pallas_call(
        _inproj_kernel,
        grid=(b, s // tm),
        in_specs=in_specs,
        out_specs=[tok(spread), tok(spread), vt_spec, tok(width), tok(width), vt_spec,
                   tok(d), tok(d), fend_spec],
        out_shape=[shp(spread), shp(spread), vt_shape, shp(width), shp(width), vt_shape,
                   shp(d), shp(d), fend_shape],
        scratch_shapes=[pltpu.VMEM((1, LANES), F32)],
        compiler_params=_params("parallel", "arbitrary"),
        name="inproj",
    )(*ins)


HEADS_PER_STEP = 8
PAIRS_PER_STEP = HEADS_PER_STEP // 2
N_FEATURES = 6


def _key_query_iota(t):
    return (lax.broadcasted_iota(jnp.int32, (t, t), 0), lax.broadcasted_iota(jnp.int32, (t, t), 1))


def _store_heads(o_ref, outs):
    for pr in range(PAIRS_PER_STEP):
        pair = jnp.concatenate([outs[2 * pr], outs[2 * pr + 1]], axis=0)
        o_ref[0, :, pr * LANES:(pr + 1) * LANES] = pair.T.astype(o_ref.dtype)


def _value_rows(vt_ref, j, h):
    return vt_ref[j, h * HEAD_DIM:(h + 1) * HEAD_DIM, :]


KV_LOOKAHEAD = 2


def _stream_kv(k_hbm, vt_hbm, k_scr, vt_scr, sem):
    bi, hg, i = pl.program_id(0), pl.program_id(1), pl.program_id(2)
    nblk, vw, t = vt_scr.shape
    kw = k_scr.shape[1]

    def copies(j):
        rows = pl.ds(pl.multiple_of(j * t, t), t)
        return (pltpu.make_async_copy(k_hbm.at[bi, rows, pl.ds(hg * kw, kw)], k_scr.at[rows, :], sem.at[0, j]),
                pltpu.make_async_copy(vt_hbm.at[bi, j, pl.ds(hg * vw, vw), :], vt_scr.at[j], sem.at[1, j]))

    @pl.when(i == 0)
    def _():
        for j in range(min(KV_LOOKAHEAD, nblk)):
            for c in copies(j):
                c.start()

    @pl.when(i + KV_LOOKAHEAD < nblk)
    def _():
        for c in copies(i + KV_LOOKAHEAD):
            c.start()

    for c in copies(i):
        c.wait()


def _fox_kernel(fend_ref, margin_ref, qf_ref, kf_hbm, vt_hbm, o_ref, m_scr, l_scr, acc_scr, s_scr,
                kf_ref, vt_ref, kv_sem):
    bi = pl.program_id(0)
    hg = pl.program_id(1)
    i = pl.program_id(2)
    t = qf_ref.shape[1]
    heads = range(HEADS_PER_STEP)
    _stream_kv(kf_hbm, vt_hbm, kf_ref, vt_ref, kv_sem)

    def tile_needed(j):
        need = False
        for h in heads:
            row = (bi * pl.num_programs(1) + hg) * HEADS_PER_STEP + h
            gap = fend_ref[row, j] - fend_ref[row, jnp.maximum(i - 1, 0)]
            need = jnp.logical_or(need, gap <= margin_ref[0])
        return need

    first = lax.while_loop(
        lambda j: jnp.logical_and(j > 0, tile_needed(jnp.maximum(j - 1, 0))), lambda j: j - 1, i)

    key, qry = _key_query_iota(t)
    causal = key <= qry
    lane = lax.broadcasted_iota(jnp.int32, (t, 2 * LANES), 1)
    q = []
    for h in heads:
        qpair = qf_ref[0, :, (h // 2) * 2 * LANES:(h // 2 + 1) * 2 * LANES]
        q_lo, f_lo = (h % 2) * HEAD_DIM, LANES + (h % 2) * N_FEATURES
        own = jnp.logical_or(jnp.logical_and(lane >= q_lo, lane < q_lo + HEAD_DIM),
                             jnp.logical_and(lane >= f_lo, lane < f_lo + N_FEATURES))
        q.append(jnp.where(own, qpair, jnp.zeros_like(qpair)))
    m_scr[...] = jnp.full_like(m_scr, -jnp.inf)
    l_scr[...] = jnp.zeros_like(l_scr)
    acc_scr[...] = jnp.zeros_like(acc_scr)

    ones_rows = jnp.ones((ONES_ROWS, t), BF16)

    def logits(j):
        start = pl.multiple_of(j * t, t)
        return [_dot_nt(kf_ref[pl.ds(start, t), (h // 2) * 2 * LANES:(h // 2 + 1) * 2 * LANES], q[h])
                for h in heads]

    def softmax_step(s, j, diagonal):
        if diagonal:
            s = [jnp.where(causal, sh, -jnp.inf) for sh in s]
        m_old = [m_scr[h] for h in heads]
        m_new = [jnp.maximum(m_old[h], jnp.max(s[h], axis=0, keepdims=True)) for h in heads]
        alpha = [jnp.exp2(m_old[h] - m_new[h]) for h in heads]
        p = [jnp.exp2(s[h] - m_new[h]).astype(BF16) for h in heads]
        pv = [_dot(jnp.concatenate([_value_rows(vt_ref, j, h), ones_rows], axis=0), p[h])
              for h in heads]
        for h in heads:
            m_scr[h] = m_new[h]
            l_scr[h] = alpha[h] * l_scr[h] + pv[h][HEAD_DIM:HEAD_DIM + 1, :]
            acc_scr[h] = alpha[h] * acc_scr[h] + pv[h][:HEAD_DIM, :]

    def visit(j, carry):
        s_next = logits(j + 1)
        softmax_step([s_scr[h] for h in heads], j, False)
        for h in heads:
            s_scr[h] = s_next[h]
        return carry

    s_first = logits(first)
    for h in heads:
        s_scr[h] = s_first[h]
    lax.fori_loop(first, i, visit, 0)
    softmax_step([s_scr[h] for h in heads], i, True)
    _store_heads(o_ref, [acc_scr[h] * (1.0 / l_scr[h]) for h in heads])


ONES_ROWS = BF16_ROWS


def _attn_scratch(t):
    n = HEADS_PER_STEP
    return [pltpu.VMEM((n, 1, t), F32), pltpu.VMEM((n, 1, t), F32), pltpu.VMEM((n, HEAD_DIM, t), F32),
            pltpu.VMEM((n, t, t), F32)]


def _kv_scratch(s, kw, nblk, vw, t):
    return [pltpu.VMEM((s, kw), BF16), pltpu.VMEM((nblk, vw, t), BF16), pltpu.SemaphoreType.DMA((2, nblk))]


def _fox(fend, margin, qf, kf, fvt):
    b, nblk, width, t = fvt.shape
    s = nblk * t
    vw = PAIRS_PER_STEP * LANES
    qw = HEADS_PER_STEP * LANES
    smem = pl.BlockSpec(memory_space=pltpu.SMEM)
    return pl.pallas_call(
        _fox_kernel,
        grid=(b, width // vw, nblk),
        in_specs=[
            smem, smem,
            pl.BlockSpec((1, t, qw), lambda bi, hg, i: (bi, i, hg)),
            _HBM, _HBM,
        ],
        out_specs=pl.BlockSpec((1, t, vw), lambda bi, hg, i: (bi, i, hg)),
        out_shape=jax.ShapeDtypeStruct((b, s, width), BF16),
        scratch_shapes=_attn_scratch(t) + _kv_scratch(s, qw, nblk, vw, t),
        compiler_params=_params("arbitrary", "arbitrary", "arbitrary"),
        name="fox_attention",
    )(fend, margin, qf, kf, fvt)


def _sb_kernel(q_ref, k_hbm, vt_hbm, u_ref, o_ref, r_scr, acc_scr, k_ref, vt_ref, kv_sem):
    i = pl.program_id(2)
    t = q_ref.shape[1]
    heads = range(HEADS_PER_STEP)
    _stream_kv(k_hbm, vt_hbm, k_ref, vt_ref, kv_sem)
    key, qry = _key_query_iota(t)
    strict = key < qry
    qlane = lax.broadcasted_iota(jnp.int32, (t, LANES), 1)
    q = []
    for h in heads:
        qpair = q_ref[0, :, (h // 2) * LANES:(h // 2 + 1) * LANES]
        q.append(jnp.where((qlane < HEAD_DIM) == (h % 2 == 0), qpair, jnp.zeros_like(qpair)))
    r_scr[...] = jnp.zeros_like(r_scr)
    acc_scr[...] = jnp.zeros_like(acc_scr)

    def logits(j):
        start = pl.multiple_of(j * t, t)
        return [_dot_nt(k_ref[pl.ds(start, t), (h // 2) * LANES:(h // 2 + 1) * LANES], q[h])
                for h in heads]

    def stick_step(z, j, diagonal):
        sp = [jnp.maximum(zh, 0.0) + jnp.log2(1.0 + jnp.exp2(-jnp.abs(zh))) for zh in z]
        log_beta = [z[h] - sp[h] for h in heads]
        if diagonal:
            sp = [jnp.where(strict, sh, 0.0) for sh in sp]
        sums = [_dot(u_ref[...], sh.astype(BF16)) for sh in sp]
        a = [jnp.exp2(log_beta[h] - sums[h][:t, :]) for h in heads]
        if diagonal:
            a = [jnp.where(strict, ah, 0.0) for ah in a]
        pv = [_dot(_value_rows(vt_ref, j, h), a[h].astype(BF16)) for h in heads]
        for h in heads:
            r = r_scr[h]
            acc_scr[h] += jnp.exp2(-r) * pv[h]
            r_scr[h] = r + sums[h][t:t + 1, :]

    def least_r():
        r = r_scr[0]
        for h in heads[1:]:
            r = jnp.minimum(r, r_scr[h])
        return jnp.min(r)

    def body(carry):
        n, _ = carry
        stick_step(logits(i - 1 - n), i - 1 - n, False)
        return n + 1, least_r()

    stick_step(logits(i), i, True)
    lax.while_loop(lambda c: jnp.logical_and(c[0] < i, c[1] < ZERO_WEIGHT_BITS), body,
                   (jnp.int32(0), least_r()))
    _store_heads(o_ref, [acc_scr[h] for h in heads])


def _sb(sq, sk, svt, later_mat):
    b, nblk, width, t = svt.shape
    s = nblk * t
    vw = PAIRS_PER_STEP * LANES
    return pl.pallas_call(
        _sb_kernel,
        grid=(b, width // vw, nblk),
        in_specs=[
            pl.BlockSpec((1, t, vw), lambda bi, hg, i: (bi, i, hg)),
            _HBM, _HBM,
            _const_spec(later_mat.shape),
        ],
        out_specs=pl.BlockSpec((1, t, vw), lambda bi, hg, i: (bi, i, hg)),
        out_shape=jax.ShapeDtypeStruct((b, s, width), BF16),
        scratch_shapes=_attn_scratch(t)[1:3] + _kv_scratch(s, vw, nblk, vw, t),
        compiler_params=_params("arbitrary", "arbitrary", "arbitrary"),
        name="sb_attention",
    )(sq, sk, svt, later_mat)


N_POST_WEIGHTS = 8


def _post_kernel(stage_shapes, x_ref, yf_ref, ys_ref, gf_ref, gs_ref, p_ref, g2_ref, gp_ref, *rest):
    w_hbm, o_ref = rest[:N_POST_WEIGHTS], rest[N_POST_WEIGHTS]
    w_vmem = rest[N_POST_WEIGHTS + 1:2 * N_POST_WEIGHTS + 1]
    dma = rest[2 * N_POST_WEIGHTS + 1:]

    @pl.when(pl.program_id(0) == 0)
    def _():
        _load_weights(w_hbm, w_vmem, stage_shapes, dma[:-1], dma[-1])

    wbf_ref, wbs_ref, wo_ref, wg_ref, wu_ref, wd_ref, wpg_ref, wpp_ref = w_vmem
    merged =(gf_ref[...].astype(F32) * _dot(yf_ref[...], wbf_ref[...])
              + gs_ref[...].astype(F32) * _dot(ys_ref[...], wbs_ref[...]))
    x = x_ref[...] + _dot(merged.astype(BF16), wo_ref[...])

    h = _rms_rows(x, g2_ref[...]).astype(BF16)
    a = _dot(h, wg_ref[...])
    u = _dot(h, wu_ref[...])
    x = x + 0.5 * _dot((a * _sigmoid(a) * u).astype(BF16), wd_ref[...])

    h = _rms_rows(x, gp_ref[...]).astype(BF16)
    gate = _sigmoid(_dot(h, wpg_ref[...]))
    o_ref[...] = x + gate * _dot(p_ref[...].astype(BF16), wpp_ref[...])


def _post(x2d, yf, ys, gf, gs, p2d, gains, weights):
    n, d = x2d.shape
    tm = POST_TILE
    assert len(weights) == N_POST_WEIGHTS

    def tok(a):
        return pl.BlockSpec((tm, a.shape[1]), lambda i: (i, 0))

    toks = [x2d, yf, ys, gf, gs, p2d]
    homes, stage_shapes, dma = _weight_scratch(weights)
    return pl.pallas_call(
        functools.partial(_post_kernel, stage_shapes),
        grid=(n // tm,),
        in_specs=[tok(a) for a in toks] + [_const_spec(g.shape) for g in gains] + [_HBM] * len(weights),
        out_specs=tok(x2d),
        out_shape=jax.ShapeDtypeStruct((n, d), F32),
        scratch_shapes=homes + dma,
        compiler_params=_params("arbitrary"),
        name="post",
    )(*toks, *gains, *weights)


@functools.lru_cache(maxsize=None)
def _layout_constants(n_heads, tile):
    width = n_heads * HEAD_DIM
    sel = np.zeros((width, LANES), np.float32)
    pq = np.zeros((3 * LANES, width), np.float32)
    pk = np.zeros((3 * LANES, width), np.float32)
    cq = np.zeros((1, width), np.float32)
    ck = np.zeros((1, width), np.float32)
    for h in range(n_heads):
        sel[h * HEAD_DIM:(h + 1) * HEAD_DIM, h] = 1.0 / HEAD_DIM
        base = (h // 2) * LANES + (h % 2) * N_FEATURES
        for part in range(3):
            pq[part * LANES + h, base + part] = 1.0
            pk[part * LANES + h, base + 3 + part] = -1.0
            cq[0, base + 3 + part] = 1.0
            ck[0, base + part] = 1.0
    selt = (sel.T > 0).astype(np.float32)
    tri = np.tril(np.ones((tile, tile), np.float32))
    return sel, selt, tri, pq, pk, cq, ck


@functools.lru_cache(maxsize=None)
def _later_matrix(tile):
    return np.concatenate([np.triu(np.ones((tile, tile), np.float32), k=1),
                           np.ones((ONES_ROWS, tile), np.float32)], axis=0)


def kernel(x, p, ffn1_norm, ffn1_w_gate, ffn1_w_up, ffn1_w_down, mix_norm, w_in, forget_bias, q_norm, k_norm, w_branch_fox, w_branch_sb, w_out, ffn2_norm, ffn2_w_gate, ffn2_w_up, ffn2_w_down, ple_norm, w_ple_gate, w_ple_proj):
    b, s, d = x.shape
    depth = w_in.shape[0]
    fox_w = w_branch_fox.shape[1]
    sb_w = w_branch_sb.shape[1]
    n_heads = forget_bias.shape[1]
    assert fox_w == n_heads * HEAD_DIM and sb_w == fox_w and n_heads <= LANES
    assert s % TOKEN_TILE == 0 and TOKEN_TILE % ATTN_TILE == 0 and TOKEN_TILE // ATTN_TILE <= SUBLANES
    assert n_heads % HEADS_PER_STEP == 0 and 2 * N_FEATURES <= LANES
    assert (b * s) % FFN_TILE == 0 and (b * s) % POST_TILE == 0

    sel, selt, tri, pq, pk, cq, ck = _layout_constants(n_heads, TOKEN_TILE)
    later = jnp.asarray(_later_matrix(ATTN_TILE), BF16)

    xf = x.reshape(b * s, d)
    for i in range(depth):
        xf = _ffn(xf, ffn1_norm[i][None], ffn1_w_gate[i], ffn1_w_up[i], ffn1_w_down[i])

        w = jnp.swapaxes(w_in[i], 0, 1)
        o_f = 3 * fox_w
        o_sb = o_f + n_heads
        o_gate = o_sb + 3 * sb_w
        wf = jnp.pad(w[o_f:o_sb], ((0, LANES - n_heads), (0, 0)))
        w_tok = jnp.concatenate([w[:2 * fox_w], w[o_sb:o_sb + 2 * sb_w], w[o_gate:], wf],
                                axis=0).astype(BF16)
        w_val = jnp.concatenate([w[2 * fox_w:o_f], w[o_sb + 2 * sb_w:o_gate]], axis=0).astype(BF16)
        fbias = jnp.pad(forget_bias[i][None], ((0, 0), (0, LANES - n_heads)))
        gq = jnp.tile(q_norm[i], n_heads)[None] * (LOG2E * HEAD_DIM ** -0.5)
        gk = jnp.tile(k_norm[i], n_heads)[None]
        consts = [mix_norm[i][None], w_tok, w_val, fbias, gq, gk,
                  jnp.asarray(sel, BF16), jnp.asarray(selt, BF16), jnp.asarray(tri, BF16),
                  jnp.asarray(pq, BF16), jnp.asarray(pk, BF16), jnp.asarray(cq), jnp.asarray(ck)]
        qf, kf, fvt, sq, sk, svt, gf, gs, fend = _inproj(xf.reshape(b, s, d), consts)

        fend = fend[:, :, :TOKEN_TILE // ATTN_TILE, :n_heads].reshape(b, s // ATTN_TILE, n_heads)
        fend = fend.transpose(0, 2, 1).reshape(b * n_heads, s // ATTN_TILE)
        ub = LOG2E * HEAD_DIM ** 0.5 * jnp.max(jnp.abs(q_norm[i])) * jnp.max(jnp.abs(k_norm[i]))
        margin = (2.0 * ub + ZERO_WEIGHT_BITS).reshape(1).astype(F32)

        y_fox = _fox(fend, margin, qf, kf, fvt)
        y_sb = _sb(sq, sk, svt, later)

        weights = [w_branch_fox[i], w_branch_sb[i], w_out[i], ffn2_w_gate[i], ffn2_w_up[i],
                   ffn2_w_down[i], w_ple_gate[i], w_ple_proj[i]]
        xf = _post(xf, y_fox.reshape(b * s, fox_w), y_sb.reshape(b * s, sb_w),
                   gf.reshape(b * s, d), gs.reshape(b * s, d), p[i].reshape(b * s, -1),
                   [ffn2_norm[i][None], ple_norm[i][None]], weights)
    return xf.reshape(b, s, d)
```

```python
import functools

import jax
import jax.numpy as jnp
import numpy as np
from jax import lax
from jax.experimental import pallas as pl
from jax.experimental.pallas import tpu as pltpu

F32 = jnp.float32
BF16 = jnp.bfloat16

EPS = 1e-6
HEAD_DIM = 64
LOG2E = 1.4426950408889634
LANES = 128
SUBLANES = 8
BF16_ROWS = 2 * SUBLANES
ZERO_WEIGHT_BITS = 152.0
VMEM_LIMIT_BYTES = 56 * 1024 * 1024

TOKEN_TILE = 512
FFN_TILE = 1024
POST_TILE = 512
ATTN_TILE = 256

NT_DIMS = (((1,), (1,)), ((), ()))


def _dot(a, b):
    return jnp.dot(a, b, preferred_element_type=F32)


def _dot_nt(a, b):
    return lax.dot_general(a, b, NT_DIMS, preferred_element_type=F32)


def _split2(a):
    hi = a.astype(BF16)
    lo = (a - hi.astype(F32)).astype(BF16)
    return hi, lo


def _split3(a):
    p1 = a.astype(BF16)
    r1 = a - p1.astype(F32)
    p2 = r1.astype(BF16)
    r2 = r1 - p2.astype(F32)
    return p1, p2, r2.astype(BF16)


def _rms_rows(x, g):
    ms = jnp.mean(x * x, axis=-1, keepdims=True)
    return x * lax.rsqrt(ms + EPS) * g


def _sigmoid(x):
    return 1.0 / (1.0 + jnp.exp(-x))


def _params(*sem):
    return pltpu.CompilerParams(dimension_semantics=sem, vmem_limit_bytes=VMEM_LIMIT_BYTES)


def _const_spec(shape):
    nd = len(shape)
    return pl.BlockSpec(shape, lambda *_: (0,) * nd, pipeline_mode=pl.Buffered(1))


STAGE_BYTES = 1024 * 1024
STAGE_SLOTS = 4


def _stage_rows(w):
    rows, cols = w.shape
    ch = max(BF16_ROWS, STAGE_BYTES // (4 * cols) // BF16_ROWS * BF16_ROWS)
    while rows % ch:
        ch -= BF16_ROWS
    return ch


def _weight_scratch(weights):
    homes = [pltpu.VMEM(w.shape, BF16) for w in weights]
    stage_shapes, dma = _stage_scratch(weights)
    return homes, stage_shapes, dma


def _stage_scratch(pieces):
    stage_shapes = sorted({(_stage_rows(w), w.shape[1]) for w in pieces})
    stages = [pltpu.VMEM((STAGE_SLOTS,) + s, F32) for s in stage_shapes]
    return stage_shapes, stages + [pltpu.SemaphoreType.DMA((len(stage_shapes), STAGE_SLOTS))]


def _load_weights(w_hbm, w_vmem, stage_shapes, stages, sem):
    jobs, used = [], [0] * len(stage_shapes)
    for src, dst in zip(w_hbm, w_vmem):
        ch = _stage_rows(src)
        k = stage_shapes.index((ch, src.shape[1]))
        for c in range(src.shape[0] // ch):
            slot = used[k] % STAGE_SLOTS
            used[k] += 1
            copy = pltpu.make_async_copy(src.at[pl.ds(c * ch, ch), :], stages[k].at[slot], sem.at[k, slot])
            jobs.append((copy, dst, c * ch, ch, stages[k], slot))
    ahead = STAGE_SLOTS - 1
    for copy, *_ in jobs[:ahead]:
        copy.start()
    for n, (copy, dst, row, ch, stage, slot) in enumerate(jobs):
        if n + ahead < len(jobs):
            jobs[n + ahead][0].start()
        copy.wait()
        dst[pl.ds(row, ch), :] = stage[slot].astype(BF16)


_HBM = pl.BlockSpec(memory_space=pl.ANY)


def _ffn_kernel(stage_shapes, x_ref, g_ref, wg_hbm, wu_hbm, wd_hbm, o_ref, wg_ref, wu_ref, wd_ref, *dma):
    @pl.when(pl.program_id(0) == 0)
    def _():
        _load_weights([wg_hbm, wu_hbm, wd_hbm], [wg_ref, wu_ref, wd_ref], stage_shapes, dma[:-1], dma[-1])

    x = x_ref[...]
    h = _rms_rows(x, g_ref[...]).astype(BF16)
    a = _dot(h, wg_ref[...])
    u = _dot(h, wu_ref[...])
    act = (a * _sigmoid(a) * u).astype(BF16)
    o_ref[...] = x + 0.5 * _dot(act, wd_ref[...])


def _ffn(x2d, g, wg, wu, wd):
    n, d = x2d.shape
    tm = FFN_TILE
    homes, stage_shapes, dma = _weight_scratch([wg, wu, wd])
    return pl.pallas_call(
        functools.partial(_ffn_kernel, stage_shapes),
        grid=(n // tm,),
        in_specs=[pl.BlockSpec((tm, d), lambda i: (i, 0)), _const_spec(g.shape), _HBM, _HBM, _HBM],
        out_specs=pl.BlockSpec((tm, d), lambda i: (i, 0)),
        out_shape=jax.ShapeDtypeStruct((n, d), F32),
        scratch_shapes=homes + dma,
        compiler_params=_params("arbitrary"),
        name="ffn",
    )(x2d, g, wg, wu, wd)


def _inproj_groups(width, d, n_heads):
    o_sb = 3 * width + n_heads
    return [(0, 0, 0, 2 * width), (o_sb, 0, 2 * width, 2 * width), (o_sb + 3 * width, 0, 4 * width, 2 * d),
            (2 * width, 1, 0, width), (o_sb + 2 * width, 1, width, width)]


def _inproj_kernel(stage_shapes, x_ref, g_ref, wt_hbm, wf_ref,
                   fbias_ref, gq_ref, gk_ref, sel_ref, selt_ref, tri_ref,
                   pq_ref, pk_ref, cq_ref, ck_ref,
                   qf_ref, kf_ref, fvt_ref, sq_ref, sk_ref, svt_ref, gf_ref, gs_ref, fend_ref,
                   carry_scr, wtok_ref, wvalt_ref, *dma):
    @pl.when(pl.program_id(1) == 0)
    def _():
        carry_scr[...] = jnp.zeros_like(carry_scr)

    t = fvt_ref.shape[3]
    n_heads = qf_ref.shape[2] // LANES
    width = sq_ref.shape[2]
    d = gf_ref.shape[2]

    @pl.when(jnp.logical_and(pl.program_id(0) == 0, pl.program_id(1) == 0))
    def _():
        homes = (wtok_ref, wvalt_ref)
        groups = _inproj_groups(width, d, n_heads)
        _load_weights([wt_hbm.at[pl.ds(src, n), :] for src, _, _, n in groups],
                      [homes[k].at[pl.ds(dst, n), :] for _, k, dst, n in groups],
                      stage_shapes, dma[:-1], dma[-1])

    h = _rms_rows(x_ref[0], g_ref[...]).astype(BF16)
    c_k, c_sb, c_gate = width, 2 * width, 4 * width

    yf = _dot_nt(h, wf_ref[...]) + fbias_ref[...]
    yg = _dot_nt(h, wtok_ref[c_gate:, :])

    lf = jnp.minimum(yf, 0.0) - jnp.log1p(jnp.exp(-jnp.abs(yf)))
    lane = lax.broadcasted_iota(jnp.int32, lf.shape, 1)
    lf = jnp.where(lane < n_heads, lf, 0.0)
    c = _dot(tri_ref[...], jnp.concatenate(_split3(lf), axis=1))

    yq = _dot_nt(h, wtok_ref[:c_k, :])
    yk = _dot_nt(h, wtok_ref[c_k:c_sb, :])
    msq = _dot((yq * yq).astype(BF16), sel_ref[...])
    msk = _dot((yk * yk).astype(BF16), sel_ref[...])

    ysb = _dot_nt(h, wtok_ref[c_sb:c_gate, :])

    def spread_rsqrt(ms):
        rhi, rlo = _split2(lax.rsqrt(ms + EPS))
        return _dot(rhi, selt_ref[...]) + _dot(rlo, selt_ref[...])

    rq = spread_rsqrt(msq)
    rk = spread_rsqrt(msk)

    fvt = _dot_nt(wvalt_ref[:width, :], h).astype(BF16)
    svt = _dot_nt(wvalt_ref[width:2 * width, :], h).astype(BF16)

    f_cum = c[:, :LANES] + c[:, LANES:2 * LANES] + c[:, 2 * LANES:] + carry_scr[...]
    carry_scr[...] = f_cum[-1:, :]
    f_bits = f_cum * LOG2E
    fparts = jnp.concatenate(_split3(f_bits), axis=1)
    featq = (_dot(fparts, pq_ref[...]) + cq_ref[...]).astype(BF16)
    featk = (_dot(fparts, pk_ref[...]) + ck_ref[...]).astype(BF16)

    yg = _sigmoid(yg)
    gf_ref[0] = yg[:, :d].astype(BF16)
    gs_ref[0] = yg[:, d:].astype(BF16)

    sq_ref[0] = (ysb[:, :width] * (LOG2E * HEAD_DIM ** -0.5)).astype(BF16)
    sk_ref[0] = ysb[:, width:].astype(BF16)

    for c in range(fvt_ref.shape[1]):
        fvt_ref[0, c] = fvt[:, c * t:(c + 1) * t]
        svt_ref[0, c] = svt[:, c * t:(c + 1) * t]

    fend_ref[0, 0] = jnp.zeros(fend_ref.shape[2:], F32)
    for c in range(fvt_ref.shape[1]):
        fend_ref[0, 0, c:c + 1, :] = f_bits[(c + 1) * t - 1:(c + 1) * t, :]

    q = (yq * rq * gq_ref[...]).astype(BF16)
    k = (yk * rk * gk_ref[...]).astype(BF16)
    for pr in range(q.shape[1] // LANES):
        src = slice(pr * LANES, (pr + 1) * LANES)
        qf_ref[0, :, 2 * pr * LANES:(2 * pr + 1) * LANES] = q[:, src]
        qf_ref[0, :, (2 * pr + 1) * LANES:(2 * pr + 2) * LANES] = featq[:, src]
        kf_ref[0, :, 2 * pr * LANES:(2 * pr + 1) * LANES] = k[:, src]
        kf_ref[0, :, (2 * pr + 1) * LANES:(2 * pr + 2) * LANES] = featk[:, src]


def _inproj(x3d, w_t, width, n_heads, consts):
    b, s, d = x3d.shape
    tm = TOKEN_TILE
    t = ATTN_TILE
    spread = 2 * width
    ins = [x3d, consts[0], w_t] + list(consts[1:])
    in_specs = [pl.BlockSpec((1, tm, d), lambda bi, i: (bi, i, 0)), _const_spec(consts[0].shape), _HBM]
    in_specs += [_const_spec(c.shape) for c in consts[1:]]
    groups = _inproj_groups(width, d, n_heads)
    stage_shapes, dma = _stage_scratch([jax.ShapeDtypeStruct((n, d), F32) for _, _, _, n in groups])
    homes = [pltpu.VMEM((4 * width + 2 * d, d), BF16), pltpu.VMEM((2 * width, d), BF16)]

    def tok(n):
        return pl.BlockSpec((1, tm, n), lambda bi, i: (bi, i, 0))

    def shp(n):
        return jax.ShapeDtypeStruct((b, s, n), BF16)

    vt_spec = pl.BlockSpec((1, tm // t, width, t), lambda bi, i: (bi, i, 0, 0))
    vt_shape = jax.ShapeDtypeStruct((b, s // t, width, t), BF16)
    fend_spec = pl.BlockSpec((1, 1, SUBLANES, LANES), lambda bi, i: (bi, i, 0, 0))
    fend_shape = jax.ShapeDtypeStruct((b, s // tm, SUBLANES, LANES), F32)

    return pl.pallas_call(
        functools.partial(_inproj_kernel, stage_shapes),
        grid=(b, s // tm),
        in_specs=in_specs,
        out_specs=[tok(spread), tok(spread), vt_spec, tok(width), tok(width), vt_spec,
                   tok(d), tok(d), fend_spec],
        out_shape=[shp(spread), shp(spread), vt_shape, shp(width), shp(width), vt_shape,
                   shp(d), shp(d), fend_shape],
        scratch_shapes=[pltpu.VMEM((1, LANES), F32)] + homes + dma,
        compiler_params=_params("arbitrary", "arbitrary"),
        name="inproj",
    )(*ins)


HEADS_PER_STEP = 8
PAIRS_PER_STEP = HEADS_PER_STEP // 2
N_FEATURES = 6


def _key_query_iota(t):
    return (lax.broadcasted_iota(jnp.int32, (t, t), 0), lax.broadcasted_iota(jnp.int32, (t, t), 1))


def _store_heads(o_ref, outs):
    for pr in range(PAIRS_PER_STEP):
        pair = jnp.concatenate([outs[2 * pr], outs[2 * pr + 1]], axis=0)
        o_ref[0, :, pr * LANES:(pr + 1) * LANES] = pair.T.astype(o_ref.dtype)


def _value_rows(vt_ref, j, h):
    return vt_ref[j, h * HEAD_DIM:(h + 1) * HEAD_DIM, :]


KV_LOOKAHEAD = 2


def _stream_kv(k_hbm, vt_hbm, k_scr, vt_scr, sem):
    bi, hg, i = pl.program_id(0), pl.program_id(1), pl.program_id(2)
    nblk, vw, t = vt_scr.shape
    kw = k_scr.shape[1]

    def copies(j):
        rows = pl.ds(pl.multiple_of(j * t, t), t)
        return (pltpu.make_async_copy(k_hbm.at[bi, rows, pl.ds(hg * kw, kw)], k_scr.at[rows, :], sem.at[0, j]),
                pltpu.make_async_copy(vt_hbm.at[bi, j, pl.ds(hg * vw, vw), :], vt_scr.at[j], sem.at[1, j]))

    @pl.when(i == 0)
    def _():
        for j in range(min(KV_LOOKAHEAD, nblk)):
            for c in copies(j):
                c.start()

    @pl.when(i + KV_LOOKAHEAD < nblk)
    def _():
        for c in copies(i + KV_LOOKAHEAD):
            c.start()

    for c in copies(i):
        c.wait()


def _fox_kernel(fend_ref, margin_ref, qf_ref, kf_hbm, vt_hbm, o_ref, m_scr, l_scr, acc_scr, s_scr,
                kf_ref, vt_ref, kv_sem):
    bi = pl.program_id(0)
    hg = pl.program_id(1)
    i = pl.program_id(2)
    t = qf_ref.shape[1]
    heads = range(HEADS_PER_STEP)
    _stream_kv(kf_hbm, vt_hbm, kf_ref, vt_ref, kv_sem)

    def tile_needed(j):
        need = False
        for h in heads:
            row = (bi * pl.num_programs(1) + hg) * HEADS_PER_STEP + h
            gap = fend_ref[row, j] - fend_ref[row, jnp.maximum(i - 1, 0)]
            need = jnp.logical_or(need, gap <= margin_ref[0])
        return need

    first = lax.while_loop(
        lambda j: jnp.logical_and(j > 0, tile_needed(jnp.maximum(j - 1, 0))), lambda j: j - 1, i)

    key, qry = _key_query_iota(t)
    causal = key <= qry
    lane = lax.broadcasted_iota(jnp.int32, (t, 2 * LANES), 1)
    q = []
    for h in heads:
        qpair = qf_ref[0, :, (h // 2) * 2 * LANES:(h // 2 + 1) * 2 * LANES]
        q_lo, f_lo = (h % 2) * HEAD_DIM, LANES + (h % 2) * N_FEATURES
        own = jnp.logical_or(jnp.logical_and(lane >= q_lo, lane < q_lo + HEAD_DIM),
                             jnp.logical_and(lane >= f_lo, lane < f_lo + N_FEATURES))
        q.append(jnp.where(own, qpair, jnp.zeros_like(qpair)))
    m_scr[...] = jnp.full_like(m_scr, -jnp.inf)
    l_scr[...] = jnp.zeros_like(l_scr)
    acc_scr[...] = jnp.zeros_like(acc_scr)

    ones_rows = jnp.ones((ONES_ROWS, t), BF16)

    def logits(j):
        start = pl.multiple_of(j * t, t)
        return [_dot_nt(kf_ref[pl.ds(start, t), (h // 2) * 2 * LANES:(h // 2 + 1) * 2 * LANES], q[h])
                for h in heads]

    def softmax_step(s, j, diagonal):
        if diagonal:
            s = [jnp.where(causal, sh, -jnp.inf) for sh in s]
        m_old = [m_scr[h] for h in heads]
        m_new = [jnp.maximum(m_old[h], jnp.max(s[h], axis=0, keepdims=True)) for h in heads]
        alpha = [jnp.exp2(m_old[h] - m_new[h]) for h in heads]
        p = [jnp.exp2(s[h] - m_new[h]).astype(BF16) for h in heads]
        pv = [_dot(jnp.concatenate([_value_rows(vt_ref, j, h), ones_rows], axis=0), p[h])
              for h in heads]
        for h in heads:
            m_scr[h] = m_new[h]
            l_scr[h] = alpha[h] * l_scr[h] + pv[h][HEAD_DIM:HEAD_DIM + 1, :]
            acc_scr[h] = alpha[h] * acc_scr[h] + pv[h][:HEAD_DIM, :]

    def visit(j, carry):
        s_next = logits(j + 1)
        softmax_step([s_scr[h] for h in heads], j, False)
        for h in heads:
            s_scr[h] = s_next[h]
        return carry

    s_first = logits(first)
    for h in heads:
        s_scr[h] = s_first[h]
    lax.fori_loop(first, i, visit, 0)
    softmax_step([s_scr[h] for h in heads], i, True)
    _store_heads(o_ref, [acc_scr[h] * (1.0 / l_scr[h]) for h in heads])


ONES_ROWS = BF16_ROWS


def _attn_scratch(t):
    n = HEADS_PER_STEP
    return [pltpu.VMEM((n, 1, t), F32), pltpu.VMEM((n, 1, t), F32), pltpu.VMEM((n, HEAD_DIM, t), F32),
            pltpu.VMEM((n, t, t), F32)]


def _kv_scratch(s, kw, nblk, vw, t):
    return [pltpu.VMEM((s, kw), BF16), pltpu.VMEM((nblk, vw, t), BF16), pltpu.SemaphoreType.DMA((2, nblk))]


def _fox(fend, margin, qf, kf, fvt):
    b, nblk, width, t = fvt.shape
    s = nblk * t
    vw = PAIRS_PER_STEP * LANES
    qw = HEADS_PER_STEP * LANES
    smem = pl.BlockSpec(memory_space=pltpu.SMEM)
    return pl.pallas_call(
        _fox_kernel,
        grid=(b, width // vw, nblk),
        in_specs=[
            smem, smem,
            pl.BlockSpec((1, t, qw), lambda bi, hg, i: (bi, i, hg)),
            _HBM, _HBM,
        ],
        out_specs=pl.BlockSpec((1, t, vw), lambda bi, hg, i: (bi, i, hg)),
        out_shape=jax.ShapeDtypeStruct((b, s, width), BF16),
        scratch_shapes=_attn_scratch(t) + _kv_scratch(s, qw, nblk, vw, t),
        compiler_params=_params("arbitrary", "arbitrary", "arbitrary"),
        name="fox_attention",
    )(fend, margin, qf, kf, fvt)


def _sb_kernel(q_ref, k_hbm, vt_hbm, u_ref, o_ref, r_scr, acc_scr, k_ref, vt_ref, kv_sem):
    i = pl.program_id(2)
    t = q_ref.shape[1]
    heads = range(HEADS_PER_STEP)
    _stream_kv(k_hbm, vt_hbm, k_ref, vt_ref, kv_sem)
    key, qry = _key_query_iota(t)
    strict = key < qry
    qlane = lax.broadcasted_iota(jnp.int32, (t, LANES), 1)
    q = []
    for h in heads:
        qpair = q_ref[0, :, (h // 2) * LANES:(h // 2 + 1) * LANES]
        q.append(jnp.where((qlane < HEAD_DIM) == (h % 2 == 0), qpair, jnp.zeros_like(qpair)))
    r_scr[...] = jnp.zeros_like(r_scr)
    acc_scr[...] = jnp.zeros_like(acc_scr)

    def logits(j):
        start = pl.multiple_of(j * t, t)
        return [_dot_nt(k_ref[pl.ds(start, t), (h // 2) * LANES:(h // 2 + 1) * LANES], q[h])
                for h in heads]

    def stick_step(z, j, diagonal):
        sp = [jnp.maximum(zh, 0.0) + jnp.log2(1.0 + jnp.exp2(-jnp.abs(zh))) for zh in z]
        log_beta = [z[h] - sp[h] for h in heads]
        if diagonal:
            sp = [jnp.where(strict, sh, 0.0) for sh in sp]
        sums = [_dot(u_ref[...], sh.astype(BF16)) for sh in sp]
        a = [jnp.exp2(log_beta[h] - sums[h][:t, :]) for h in heads]
        if diagonal:
            a = [jnp.where(strict, ah, 0.0) for ah in a]
        pv = [_dot(_value_rows(vt_ref, j, h), a[h].astype(BF16)) for h in heads]
        for h in heads:
            r = r_scr[h]
            acc_scr[h] += jnp.exp2(-r) * pv[h]
            r_scr[h] = r + sums[h][t:t + 1, :]

    def least_r():
        r = r_scr[0]
        for h in heads[1:]:
            r = jnp.minimum(r, r_scr[h])
        return jnp.min(r)

    def body(carry):
        n, _ = carry
        stick_step(logits(i - 1 - n), i - 1 - n, False)
        return n + 1, least_r()

    stick_step(logits(i), i, True)
    lax.while_loop(lambda c: jnp.logical_and(c[0] < i, c[1] < ZERO_WEIGHT_BITS), body,
                   (jnp.int32(0), least_r()))
    _store_heads(o_ref, [acc_scr[h] for h in heads])


def _sb(sq, sk, svt, later_mat):
    b, nblk, width, t = svt.shape
    s = nblk * t
    vw = PAIRS_PER_STEP * LANES
    return pl.pallas_call(
        _sb_kernel,
        grid=(b, width // vw, nblk),
        in_specs=[
            pl.BlockSpec((1, t, vw), lambda bi, hg, i: (bi, i, hg)),
            _HBM, _HBM,
            _const_spec(later_mat.shape),
        ],
        out_specs=pl.BlockSpec((1, t, vw), lambda bi, hg, i: (bi, i, hg)),
        out_shape=jax.ShapeDtypeStruct((b, s, width), BF16),
        scratch_shapes=_attn_scratch(t)[1:3] + _kv_scratch(s, vw, nblk, vw, t),
        compiler_params=_params("arbitrary", "arbitrary", "arbitrary"),
        name="sb_attention",
    )(sq, sk, svt, later_mat)


N_POST_WEIGHTS = 8


def _post_kernel(stage_shapes, x_ref, yf_ref, ys_ref, gf_ref, gs_ref, p_ref, g2_ref, gp_ref, *rest):
    w_hbm, o_ref = rest[:N_POST_WEIGHTS], rest[N_POST_WEIGHTS]
    w_vmem = rest[N_POST_WEIGHTS + 1:2 * N_POST_WEIGHTS + 1]
    dma = rest[2 * N_POST_WEIGHTS + 1:]

    @pl.when(pl.program_id(0) == 0)
    def _():
        _load_weights(w_hbm, w_vmem, stage_shapes, dma[:-1], dma[-1])

    wbf_ref, wbs_ref, wo_ref, wg_ref, wu_ref, wd_ref, wpg_ref, wpp_ref = w_vmem
    merged =(gf_ref[...].astype(F32) * _dot(yf_ref[...], wbf_ref[...])
              + gs_ref[...].astype(F32) * _dot(ys_ref[...], wbs_ref[...]))
    x = x_ref[...] + _dot(merged.astype(BF16), wo_ref[...])

    h = _rms_rows(x, g2_ref[...]).astype(BF16)
    a = _dot(h, wg_ref[...])
    u = _dot(h, wu_ref[...])
    x = x + 0.5 * _dot((a * _sigmoid(a) * u).astype(BF16), wd_ref[...])

    h = _rms_rows(x, gp_ref[...]).astype(BF16)
    gate = _sigmoid(_dot(h, wpg_ref[...]))
    o_ref[...] = x + gate * _dot(p_ref[...].astype(BF16), wpp_ref[...])


def _post(x2d, yf, ys, gf, gs, p2d, gains, weights):
    n, d = x2d.shape
    tm = POST_TILE
    assert len(weights) == N_POST_WEIGHTS

    def tok(a):
        return pl.BlockSpec((tm, a.shape[1]), lambda i: (i, 0))

    toks = [x2d, yf, ys, gf, gs, p2d]
    homes, stage_shapes, dma = _weight_scratch(weights)
    return pl.pallas_call(
        functools.partial(_post_kernel, stage_shapes),
        grid=(n // tm,),
        in_specs=[tok(a) for a in toks] + [_const_spec(g.shape) for g in gains] + [_HBM] * len(weights),
        out_specs=tok(x2d),
        out_shape=jax.ShapeDtypeStruct((n, d), F32),
        scratch_shapes=homes + dma,
        compiler_params=_params("arbitrary"),
        name="post",
    )(*toks, *gains, *weights)


@functools.lru_cache(maxsize=None)
def _layout_constants(n_heads, tile):
    width = n_heads * HEAD_DIM
    sel = np.zeros((width, LANES), np.float32)
    pq = np.zeros((3 * LANES, width), np.float32)
    pk = np.zeros((3 * LANES, width), np.float32)
    cq = np.zeros((1, width), np.float32)
    ck = np.zeros((1, width), np.float32)
    for h in range(n_heads):
        sel[h * HEAD_DIM:(h + 1) * HEAD_DIM, h] = 1.0 / HEAD_DIM
        base = (h // 2) * LANES + (h % 2) * N_FEATURES
        for part in range(3):
            pq[part * LANES + h, base + part] = 1.0
            pk[part * LANES + h, base + 3 + part] = -1.0
            cq[0, base + 3 + part] = 1.0
            ck[0, base + part] = 1.0
    selt = (sel.T > 0).astype(np.float32)
    tri = np.tril(np.ones((tile, tile), np.float32))
    return sel, selt, tri, pq, pk, cq, ck


@functools.lru_cache(maxsize=None)
def _later_matrix(tile):
    return np.concatenate([np.triu(np.ones((tile, tile), np.float32), k=1),
                           np.ones((ONES_ROWS, tile), np.float32)], axis=0)


def kernel(x, p, ffn1_norm, ffn1_w_gate, ffn1_w_up, ffn1_w_down, mix_norm, w_in, forget_bias, q_norm, k_norm, w_branch_fox, w_branch_sb, w_out, ffn2_norm, ffn2_w_gate, ffn2_w_up, ffn2_w_down, ple_norm, w_ple_gate, w_ple_proj):
    b, s, d = x.shape
    depth = w_in.shape[0]
    fox_w = w_branch_fox.shape[1]
    sb_w = w_branch_sb.shape[1]
    n_heads = forget_bias.shape[1]
    assert fox_w == n_heads * HEAD_DIM and sb_w == fox_w and n_heads <= LANES
    assert s % TOKEN_TILE == 0 and TOKEN_TILE % ATTN_TILE == 0 and TOKEN_TILE // ATTN_TILE <= SUBLANES
    assert n_heads % HEADS_PER_STEP == 0 and 2 * N_FEATURES <= LANES
    assert (b * s) % FFN_TILE == 0 and (b * s) % POST_TILE == 0

    sel, selt, tri, pq, pk, cq, ck = _layout_constants(n_heads, TOKEN_TILE)
    later = jnp.asarray(_later_matrix(ATTN_TILE), BF16)

    xf = x.reshape(b * s, d)
    for i in range(depth):
        xf = _ffn(xf, ffn1_norm[i][None], ffn1_w_gate[i], ffn1_w_up[i], ffn1_w_down[i])

        w = jnp.swapaxes(w_in[i], 0, 1)
        o_f = 3 * fox_w
        wf = jnp.pad(w[o_f:o_f + n_heads], ((0, LANES - n_heads), (0, 0))).astype(BF16)
        fbias = jnp.pad(forget_bias[i][None], ((0, 0), (0, LANES - n_heads)))
        gq = jnp.tile(q_norm[i], n_heads)[None] * (LOG2E * HEAD_DIM ** -0.5)
        gk = jnp.tile(k_norm[i], n_heads)[None]
        consts = [mix_norm[i][None], wf, fbias, gq, gk,
                  jnp.asarray(sel, BF16), jnp.asarray(selt, BF16), jnp.asarray(tri, BF16),
                  jnp.asarray(pq, BF16), jnp.asarray(pk, BF16), jnp.asarray(cq), jnp.asarray(ck)]
        qf, kf, fvt, sq, sk, svt, gf, gs, fend = _inproj(xf.reshape(b, s, d), w, fox_w, n_heads, consts)

        fend = fend[:, :, :TOKEN_TILE // ATTN_TILE, :n_heads].reshape(b, s // ATTN_TILE, n_heads)
        fend = fend.transpose(0, 2, 1).reshape(b * n_heads, s // ATTN_TILE)
        ub = LOG2E * HEAD_DIM ** 0.5 * jnp.max(jnp.abs(q_norm[i])) * jnp.max(jnp.abs(k_norm[i]))
        margin = (2.0 * ub + ZERO_WEIGHT_BITS).reshape(1).astype(F32)

        y_fox = _fox(fend, margin, qf, kf, fvt)
        y_sb = _sb(sq, sk, svt, later)

        weights = [w_branch_fox[i], w_branch_sb[i], w_out[i], ffn2_w_gate[i], ffn2_w_up[i],
                   ffn2_w_down[i], w_ple_gate[i], w_ple_proj[i]]
        xf = _post(xf, y_fox.reshape(b * s, fox_w), y_sb.reshape(b * s, sb_w),
                   gf.reshape(b * s, d), gs.reshape(b * s, d), p[i].reshape(b * s, -1),
                   [ffn2_norm[i][None], ple_norm[i][None]], weights)
    return xf.reshape(b, s, d)
```

```python
import functools

import jax
import jax.numpy as jnp
import numpy as np
from jax import lax
from jax.experimental import pallas as pl
from jax.experimental.pallas import tpu as pltpu

F32 = jnp.float32
BF16 = jnp.bfloat16

EPS = 1e-6
HEAD_DIM = 64
LOG2E = 1.4426950408889634
LANES = 128
SUBLANES = 8
BF16_ROWS = 2 * SUBLANES
ZERO_WEIGHT_BITS = 152.0
VMEM_LIMIT_BYTES = 56 * 1024 * 1024

TOKEN_TILE = 512
FFN_TILE = 1024
POST_TILE = 512
ATTN_TILE = 256

NT_DIMS = (((1,), (1,)), ((), ()))


def _dot(a, b):
    return jnp.dot(a, b, preferred_element_type=F32)


def _dot_nt(a, b):
    return lax.dot_general(a, b, NT_DIMS, preferred_element_type=F32)


def _split2(a):
    hi = a.astype(BF16)
    lo = (a - hi.astype(F32)).astype(BF16)
    return hi, lo


def _split3(a):
    p1 = a.astype(BF16)
    r1 = a - p1.astype(F32)
    p2 = r1.astype(BF16)
    r2 = r1 - p2.astype(F32)
    return p1, p2, r2.astype(BF16)


def _rms_rows(x, g):
    ms = jnp.mean(x * x, axis=-1, keepdims=True)
    return x * lax.rsqrt(ms + EPS) * g


def _sigmoid(x):
    return 1.0 / (1.0 + jnp.exp(-x))


def _params(*sem):
    return pltpu.CompilerParams(dimension_semantics=sem, vmem_limit_bytes=VMEM_LIMIT_BYTES)


def _const_spec(shape):
    nd = len(shape)
    return pl.BlockSpec(shape, lambda *_: (0,) * nd, pipeline_mode=pl.Buffered(1))


STAGE_BYTES = 1024 * 1024
STAGE_SLOTS = 4


def _stage_rows(w):
    rows, cols = w.shape
    ch = max(BF16_ROWS, STAGE_BYTES // (4 * cols) // BF16_ROWS * BF16_ROWS)
    while rows % ch:
        ch -= BF16_ROWS
    return ch


def _weight_scratch(weights):
    homes = [pltpu.VMEM(w.shape, BF16) for w in weights]
    stage_shapes, dma = _stage_scratch(weights)
    return homes, stage_shapes, dma


def _stage_scratch(pieces):
    stage_shapes = sorted({(_stage_rows(w), w.shape[1]) for w in pieces})
    stages = [pltpu.VMEM((STAGE_SLOTS,) + s, F32) for s in stage_shapes]
    return stage_shapes, stages + [pltpu.SemaphoreType.DMA((len(stage_shapes), STAGE_SLOTS))]


def _load_weights(w_hbm, w_vmem, stage_shapes, stages, sem):
    jobs, used = [], [0] * len(stage_shapes)
    for src, dst in zip(w_hbm, w_vmem):
        ch = _stage_rows(src)
        k = stage_shapes.index((ch, src.shape[1]))
        for c in range(src.shape[0] // ch):
            slot = used[k] % STAGE_SLOTS
            used[k] += 1
            copy = pltpu.make_async_copy(src.at[pl.ds(c * ch, ch), :], stages[k].at[slot], sem.at[k, slot])
            jobs.append((copy, dst, c * ch, ch, stages[k], slot))
    ahead = STAGE_SLOTS - 1
    for n, (copy, *_) in enumerate(jobs[:ahead]):
        copy.start(priority=n % 2)
    for n, (copy, dst, row, ch, stage, slot) in enumerate(jobs):
        if n + ahead < len(jobs):
            jobs[n + ahead][0].start(priority=(n + ahead) % 2)
        copy.wait()
        dst[pl.ds(row, ch), :] = stage[slot].astype(BF16)


_HBM = pl.BlockSpec(memory_space=pl.ANY)


def _ffn_kernel(stage_shapes, x_ref, g_ref, wg_hbm, wu_hbm, wd_hbm, o_ref, wg_ref, wu_ref, wd_ref, *dma):
    @pl.when(pl.program_id(0) == 0)
    def _():
        _load_weights([wg_hbm, wu_hbm, wd_hbm], [wg_ref, wu_ref, wd_ref], stage_shapes, dma[:-1], dma[-1])

    x = x_ref[...]
    h = _rms_rows(x, g_ref[...]).astype(BF16)
    a = _dot(h, wg_ref[...])
    u = _dot(h, wu_ref[...])
    act = (a * _sigmoid(a) * u).astype(BF16)
    o_ref[...] = x + 0.5 * _dot(act, wd_ref[...])


def _ffn(x2d, g, wg, wu, wd):
    n, d = x2d.shape
    tm = FFN_TILE
    homes, stage_shapes, dma = _weight_scratch([wg, wu, wd])
    return pl.pallas_call(
        functools.partial(_ffn_kernel, stage_shapes),
        grid=(n // tm,),
        in_specs=[pl.BlockSpec((tm, d), lambda i: (i, 0)), _const_spec(g.shape), _HBM, _HBM, _HBM],
        out_specs=pl.BlockSpec((tm, d), lambda i: (i, 0)),
        out_shape=jax.ShapeDtypeStruct((n, d), F32),
        scratch_shapes=homes + dma,
        compiler_params=_params("arbitrary"),
        name="ffn",
    )(x2d, g, wg, wu, wd)


def _inproj_groups(width, d, n_heads):
    o_sb = 3 * width + n_heads
    return [(0, 0, 0, 2 * width), (o_sb, 0, 2 * width, 2 * width), (o_sb + 3 * width, 0, 4 * width, 2 * d),
            (2 * width, 1, 0, width), (o_sb + 2 * width, 1, width, width)]


def _inproj_kernel(stage_shapes, x_ref, g_ref, wt_hbm, wf_ref,
                   fbias_ref, gq_ref, gk_ref, sel_ref, selt_ref, tri_ref,
                   pq_ref, pk_ref, cq_ref, ck_ref,
                   qf_ref, kf_ref, fvt_ref, sq_ref, sk_ref, svt_ref, gf_ref, gs_ref, fend_ref,
                   carry_scr, wtok_ref, wvalt_ref, *dma):
    @pl.when(pl.program_id(1) == 0)
    def _():
        carry_scr[...] = jnp.zeros_like(carry_scr)

    t = fvt_ref.shape[3]
    n_heads = qf_ref.shape[2] // LANES
    width = sq_ref.shape[2]
    d = gf_ref.shape[2]

    @pl.when(jnp.logical_and(pl.program_id(0) == 0, pl.program_id(1) == 0))
    def _():
        homes = (wtok_ref, wvalt_ref)
        groups = _inproj_groups(width, d, n_heads)
        _load_weights([wt_hbm.at[pl.ds(src, n), :] for src, _, _, n in groups],
                      [homes[k].at[pl.ds(dst, n), :] for _, k, dst, n in groups],
                      stage_shapes, dma[:-1], dma[-1])

    h = _rms_rows(x_ref[0], g_ref[...]).astype(BF16)
    c_k, c_sb, c_gate = width, 2 * width, 4 * width

    yf = _dot_nt(h, wf_ref[...]) + fbias_ref[...]
    yg = _dot_nt(h, wtok_ref[c_gate:, :])

    lf = jnp.minimum(yf, 0.0) - jnp.log1p(jnp.exp(-jnp.abs(yf)))
    lane = lax.broadcasted_iota(jnp.int32, lf.shape, 1)
    lf = jnp.where(lane < n_heads, lf, 0.0)
    c = _dot(tri_ref[...], jnp.concatenate(_split3(lf), axis=1))

    yq = _dot_nt(h, wtok_ref[:c_k, :])
    yk = _dot_nt(h, wtok_ref[c_k:c_sb, :])
    msq = _dot((yq * yq).astype(BF16), sel_ref[...])
    msk = _dot((yk * yk).astype(BF16), sel_ref[...])

    ysb = _dot_nt(h, wtok_ref[c_sb:c_gate, :])

    def spread_rsqrt(ms):
        rhi, rlo = _split2(lax.rsqrt(ms + EPS))
        return _dot(rhi, selt_ref[...]) + _dot(rlo, selt_ref[...])

    rq = spread_rsqrt(msq)
    rk = spread_rsqrt(msk)

    fvt = _dot_nt(wvalt_ref[:width, :], h).astype(BF16)
    svt = _dot_nt(wvalt_ref[width:2 * width, :], h).astype(BF16)

    f_cum = c[:, :LANES] + c[:, LANES:2 * LANES] + c[:, 2 * LANES:] + carry_scr[...]
    carry_scr[...] = f_cum[-1:, :]
    f_bits = f_cum * LOG2E
    fparts = jnp.concatenate(_split3(f_bits), axis=1)
    featq = (_dot(fparts, pq_ref[...]) + cq_ref[...]).astype(BF16)
    featk = (_dot(fparts, pk_ref[...]) + ck_ref[...]).astype(BF16)

    yg = _sigmoid(yg)
    gf_ref[0] = yg[:, :d].astype(BF16)
    gs_ref[0] = yg[:, d:].astype(BF16)

    sq_ref[0] = (ysb[:, :width] * (LOG2E * HEAD_DIM ** -0.5)).astype(BF16)
    sk_ref[0] = ysb[:, width:].astype(BF16)

    for c in range(fvt_ref.shape[1]):
        fvt_ref[0, c] = fvt[:, c * t:(c + 1) * t]
        svt_ref[0, c] = svt[:, c * t:(c + 1) * t]

    fend_ref[0, 0] = jnp.zeros(fend_ref.shape[2:], F32)
    for c in range(fvt_ref.shape[1]):
        fend_ref[0, 0, c:c + 1, :] = f_bits[(c + 1) * t - 1:(c + 1) * t, :]

    q = (yq * rq * gq_ref[...]).astype(BF16)
    k = (yk * rk * gk_ref[...]).astype(BF16)
    for pr in range(q.shape[1] // LANES):
        src = slice(pr * LANES, (pr + 1) * LANES)
        qf_ref[0, :, 2 * pr * LANES:(2 * pr + 1) * LANES] = q[:, src]
        qf_ref[0, :, (2 * pr + 1) * LANES:(2 * pr + 2) * LANES] = featq[:, src]
        kf_ref[0, :, 2 * pr * LANES:(2 * pr + 1) * LANES] = k[:, src]
        kf_ref[0, :, (2 * pr + 1) * LANES:(2 * pr + 2) * LANES] = featk[:, src]


def _inproj(x3d, w_t, width, n_heads, consts):
    b, s, d = x3d.shape
    tm = TOKEN_TILE
    t = ATTN_TILE
    spread = 2 * width
    ins = [x3d, consts[0], w_t] + list(consts[1:])
    in_specs = [pl.BlockSpec((1, tm, d), lambda bi, i: (bi, i, 0)), _const_spec(consts[0].shape), _HBM]
    in_specs += [_const_spec(c.shape) for c in consts[1:]]
    groups = _inproj_groups(width, d, n_heads)
    stage_shapes, dma = _stage_scratch([jax.ShapeDtypeStruct((n, d), F32) for _, _, _, n in groups])
    homes = [pltpu.VMEM((4 * width + 2 * d, d), BF16), pltpu.VMEM((2 * width, d), BF16)]

    def tok(n):
        return pl.BlockSpec((1, tm, n), lambda bi, i: (bi, i, 0))

    def shp(n):
        return jax.ShapeDtypeStruct((b, s, n), BF16)

    vt_spec = pl.BlockSpec((1, tm // t, width, t), lambda bi, i: (bi, i, 0, 0))
    vt_shape = jax.ShapeDtypeStruct((b, s // t, width, t), BF16)
    fend_spec = pl.BlockSpec((1, 1, SUBLANES, LANES), lambda bi, i: (bi, i, 0, 0))
    fend_shape = jax.ShapeDtypeStruct((b, s // tm, SUBLANES, LANES), F32)

    return pl.pallas_call(
        functools.partial(_inproj_kernel, stage_shapes),
        grid=(b, s // tm),
        in_specs=in_specs,
        out_specs=[tok(spread), tok(spread), vt_spec, tok(width), tok(width), vt_spec,
                   tok(d), tok(d), fend_spec],
        out_shape=[shp(spread), shp(spread), vt_shape, shp(width), shp(width), vt_shape,
                   shp(d), shp(d), fend_shape],
        scratch_shapes=[pltpu.VMEM((1, LANES), F32)] + homes + dma,
        compiler_params=_params("arbitrary", "arbitrary"),
        name="inproj",
    )(*ins)


HEADS_PER_STEP = 8
PAIRS_PER_STEP = HEADS_PER_STEP // 2
N_FEATURES = 6


def _key_query_iota(t):
    return (lax.broadcasted_iota(jnp.int32, (t, t), 0), lax.broadcasted_iota(jnp.int32, (t, t), 1))


def _store_heads(o_ref, outs):
    for pr in range(PAIRS_PER_STEP):
        pair = jnp.concatenate([outs[2 * pr], outs[2 * pr + 1]], axis=0)
        o_ref[0, :, pr * LANES:(pr + 1) * LANES] = pair.T.astype(o_ref.dtype)


def _value_rows(vt_ref, j, h):
    return vt_ref[j, h * HEAD_DIM:(h + 1) * HEAD_DIM, :]


KV_LOOKAHEAD = 2


def _stream_kv(k_hbm, vt_hbm, k_scr, vt_scr, sem):
    bi, hg, i = pl.program_id(0), pl.program_id(1), pl.program_id(2)
    nblk, vw, t = vt_scr.shape
    kw = k_scr.shape[1]

    def copies(j):
        rows = pl.ds(pl.multiple_of(j * t, t), t)
        return (pltpu.make_async_copy(k_hbm.at[bi, rows, pl.ds(hg * kw, kw)], k_scr.at[rows, :], sem.at[0, j]),
                pltpu.make_async_copy(vt_hbm.at[bi, j, pl.ds(hg * vw, vw), :], vt_scr.at[j], sem.at[1, j]))

    @pl.when(i == 0)
    def _():
        for j in range(min(KV_LOOKAHEAD, nblk)):
            for c in copies(j):
                c.start()

    @pl.when(i + KV_LOOKAHEAD < nblk)
    def _():
        for c in copies(i + KV_LOOKAHEAD):
            c.start()

    for c in copies(i):
        c.wait()


def _fox_kernel(fend_ref, margin_ref, qf_ref, kf_hbm, vt_hbm, o_ref, m_scr, l_scr, acc_scr, s_scr,
                kf_ref, vt_ref, kv_sem):
    bi = pl.program_id(0)
    hg = pl.program_id(1)
    i = pl.program_id(2)
    t = qf_ref.shape[1]
    heads = range(HEADS_PER_STEP)
    _stream_kv(kf_hbm, vt_hbm, kf_ref, vt_ref, kv_sem)

    def tile_needed(j):
        need = False
        for h in heads:
            row = (bi * pl.num_programs(1) + hg) * HEADS_PER_STEP + h
            gap = fend_ref[row, j] - fend_ref[row, jnp.maximum(i - 1, 0)]
            need = jnp.logical_or(need, gap <= margin_ref[0])
        return need

    first = lax.while_loop(
        lambda j: jnp.logical_and(j > 0, tile_needed(jnp.maximum(j - 1, 0))), lambda j: j - 1, i)

    key, qry = _key_query_iota(t)
    causal = key <= qry
    lane = lax.broadcasted_iota(jnp.int32, (t, 2 * LANES), 1)
    q = []
    for h in heads:
        qpair = qf_ref[0, :, (h // 2) * 2 * LANES:(h // 2 + 1) * 2 * LANES]
        q_lo, f_lo = (h % 2) * HEAD_DIM, LANES + (h % 2) * N_FEATURES
        own = jnp.logical_or(jnp.logical_and(lane >= q_lo, lane < q_lo + HEAD_DIM),
                             jnp.logical_and(lane >= f_lo, lane < f_lo + N_FEATURES))
        q.append(jnp.where(own, qpair, jnp.zeros_like(qpair)))
    m_scr[...] = jnp.full_like(m_scr, -jnp.inf)
    l_scr[...] = jnp.zeros_like(l_scr)
    acc_scr[...] = jnp.zeros_like(acc_scr)

    ones_rows = jnp.ones((ONES_ROWS, t), BF16)

    def logits(j):
        start = pl.multiple_of(j * t, t)
        return [_dot_nt(kf_ref[pl.ds(start, t), (h // 2) * 2 * LANES:(h // 2 + 1) * 2 * LANES], q[h])
                for h in heads]

    def softmax_step(s, j, diagonal):
        if diagonal:
            s = [jnp.where(causal, sh, -jnp.inf) for sh in s]
        m_old = [m_scr[h] for h in heads]
        m_new = [jnp.maximum(m_old[h], jnp.max(s[h], axis=0, keepdims=True)) for h in heads]
        alpha = [jnp.exp2(m_old[h] - m_new[h]) for h in heads]
        p = [jnp.exp2(s[h] - m_new[h]).astype(BF16) for h in heads]
        pv = [_dot(jnp.concatenate([_value_rows(vt_ref, j, h), ones_rows], axis=0), p[h])
              for h in heads]
        for h in heads:
            m_scr[h] = m_new[h]
            l_scr[h] = alpha[h] * l_scr[h] + pv[h][HEAD_DIM:HEAD_DIM + 1, :]
            acc_scr[h] = alpha[h] * acc_scr[h] + pv[h][:HEAD_DIM, :]

    def visit(j, carry):
        s_next = logits(j + 1)
        softmax_step([s_scr[h] for h in heads], j, False)
        for h in heads:
            s_scr[h] = s_next[h]
        return carry

    s_first = logits(first)
    for h in heads:
        s_scr[h] = s_first[h]
    lax.fori_loop(first, i, visit, 0)
    softmax_step([s_scr[h] for h in heads], i, True)
    _store_heads(o_ref, [acc_scr[h] * (1.0 / l_scr[h]) for h in heads])


ONES_ROWS = BF16_ROWS


def _attn_scratch(t):
    n = HEADS_PER_STEP
    return [pltpu.VMEM((n, 1, t), F32), pltpu.VMEM((n, 1, t), F32), pltpu.VMEM((n, HEAD_DIM, t), F32),
            pltpu.VMEM((n, t, t), F32)]


def _kv_scratch(s, kw, nblk, vw, t):
    return [pltpu.VMEM((s, kw), BF16), pltpu.VMEM((nblk, vw, t), BF16), pltpu.SemaphoreType.DMA((2, nblk))]


def _fox(fend, margin, qf, kf, fvt):
    b, nblk, width, t = fvt.shape
    s = nblk * t
    vw = PAIRS_PER_STEP * LANES
    qw = HEADS_PER_STEP * LANES
    smem = pl.BlockSpec(memory_space=pltpu.SMEM)
    return pl.pallas_call(
        _fox_kernel,
        grid=(b, width // vw, nblk),
        in_specs=[
            smem, smem,
            pl.BlockSpec((1, t, qw), lambda bi, hg, i: (bi, i, hg)),
            _HBM, _HBM,
        ],
        out_specs=pl.BlockSpec((1, t, vw), lambda bi, hg, i: (bi, i, hg)),
        out_shape=jax.ShapeDtypeStruct((b, s, width), BF16),
        scratch_shapes=_attn_scratch(t) + _kv_scratch(s, qw, nblk, vw, t),
        compiler_params=_params("arbitrary", "arbitrary", "arbitrary"),
        name="fox_attention",
    )(fend, margin, qf, kf, fvt)


def _sb_kernel(q_ref, k_hbm, vt_hbm, u_ref, o_ref, r_scr, acc_scr, k_ref, vt_ref, kv_sem):
    i = pl.program_id(2)
    t = q_ref.shape[1]
    heads = range(HEADS_PER_STEP)
    _stream_kv(k_hbm, vt_hbm, k_ref, vt_ref, kv_sem)
    key, qry = _key_query_iota(t)
    strict = key < qry
    qlane = lax.broadcasted_iota(jnp.int32, (t, LANES), 1)
    q = []
    for h in heads:
        qpair = q_ref[0, :, (h // 2) * LANES:(h // 2 + 1) * LANES]
        q.append(jnp.where((qlane < HEAD_DIM) == (h % 2 == 0), qpair, jnp.zeros_like(qpair)))
    r_scr[...] = jnp.zeros_like(r_scr)
    acc_scr[...] = jnp.zeros_like(acc_scr)

    def logits(j):
        start = pl.multiple_of(j * t, t)
        return [_dot_nt(k_ref[pl.ds(start, t), (h // 2) * LANES:(h // 2 + 1) * LANES], q[h])
                for h in heads]

    def stick_step(z, j, diagonal):
        sp = [jnp.maximum(zh, 0.0) + jnp.log2(1.0 + jnp.exp2(-jnp.abs(zh))) for zh in z]
        log_beta = [z[h] - sp[h] for h in heads]
        if diagonal:
            sp = [jnp.where(strict, sh, 0.0) for sh in sp]
        sums = [_dot(u_ref[...], sh.astype(BF16)) for sh in sp]
        a = [jnp.exp2(log_beta[h] - sums[h][:t, :]) for h in heads]
        if diagonal:
            a = [jnp.where(strict, ah, 0.0) for ah in a]
        pv = [_dot(_value_rows(vt_ref, j, h), a[h].astype(BF16)) for h in heads]
        for h in heads:
            r = r_scr[h]
            acc_scr[h] += jnp.exp2(-r) * pv[h]
            r_scr[h] = r + sums[h][t:t + 1, :]

    def least_r():
        r = r_scr[0]
        for h in heads[1:]:
            r = jnp.minimum(r, r_scr[h])
        return jnp.min(r)

    def body(carry):
        n, _ = carry
        stick_step(logits(i - 1 - n), i - 1 - n, False)
        return n + 1, least_r()

    stick_step(logits(i), i, True)
    lax.while_loop(lambda c: jnp.logical_and(c[0] < i, c[1] < ZERO_WEIGHT_BITS), body,
                   (jnp.int32(0), least_r()))
    _store_heads(o_ref, [acc_scr[h] for h in heads])


def _sb(sq, sk, svt, later_mat):
    b, nblk, width, t = svt.shape
    s = nblk * t
    vw = PAIRS_PER_STEP * LANES
    return pl.pallas_call(
        _sb_kernel,
        grid=(b, width // vw, nblk),
        in_specs=[
            pl.BlockSpec((1, t, vw), lambda bi, hg, i: (bi, i, hg)),
            _HBM, _HBM,
            _const_spec(later_mat.shape),
        ],
        out_specs=pl.BlockSpec((1, t, vw), lambda bi, hg, i: (bi, i, hg)),
        out_shape=jax.ShapeDtypeStruct((b, s, width), BF16),
        scratch_shapes=_attn_scratch(t)[1:3] + _kv_scratch(s, vw, nblk, vw, t),
        compiler_params=_params("arbitrary", "arbitrary", "arbitrary"),
        name="sb_attention",
    )(sq, sk, svt, later_mat)


N_POST_WEIGHTS = 8


def _post_kernel(stage_shapes, x_ref, yf_ref, ys_ref, gf_ref, gs_ref, p_ref, g2_ref, gp_ref, *rest):
    w_hbm, o_ref = rest[:N_POST_WEIGHTS], rest[N_POST_WEIGHTS]
    w_vmem = rest[N_POST_WEIGHTS + 1:2 * N_POST_WEIGHTS + 1]
    dma = rest[2 * N_POST_WEIGHTS + 1:]

    @pl.when(pl.program_id(0) == 0)
    def _():
        _load_weights(w_hbm, w_vmem, stage_shapes, dma[:-1], dma[-1])

    wbf_ref, wbs_ref, wo_ref, wg_ref, wu_ref, wd_ref, wpg_ref, wpp_ref = w_vmem
    merged =(gf_ref[...].astype(F32) * _dot(yf_ref[...], wbf_ref[...])
              + gs_ref[...].astype(F32) * _dot(ys_ref[...], wbs_ref[...]))
    x = x_ref[...] + _dot(merged.astype(BF16), wo_ref[...])

    h = _rms_rows(x, g2_ref[...]).astype(BF16)
    a = _dot(h, wg_ref[...])
    u = _dot(h, wu_ref[...])
    x = x + 0.5 * _dot((a * _sigmoid(a) * u).astype(BF16), wd_ref[...])

    h = _rms_rows(x, gp_ref[...]).astype(BF16)
    gate = _sigmoid(_dot(h, wpg_ref[...]))
    o_ref[...] = x + gate * _dot(p_ref[...].astype(BF16), wpp_ref[...])


def _post(x2d, yf, ys, gf, gs, p2d, gains, weights):
    n, d = x2d.shape
    tm = POST_TILE
    assert len(weights) == N_POST_WEIGHTS

    def tok(a):
        return pl.BlockSpec((tm, a.shape[1]), lambda i: (i, 0))

    toks = [x2d, yf, ys, gf, gs, p2d]
    homes, stage_shapes, dma = _weight_scratch(weights)
    return pl.pallas_call(
        functools.partial(_post_kernel, stage_shapes),
        grid=(n // tm,),
        in_specs=[tok(a) for a in toks] + [_const_spec(g.shape) for g in gains] + [_HBM] * len(weights),
        out_specs=tok(x2d),
        out_shape=jax.ShapeDtypeStruct((n, d), F32),
        scratch_shapes=homes + dma,
        compiler_params=_params("arbitrary"),
        name="post",
    )(*toks, *gains, *weights)


@functools.lru_cache(maxsize=None)
def _layout_constants(n_heads, tile):
    width = n_heads * HEAD_DIM
    sel = np.zeros((width, LANES), np.float32)
    pq = np.zeros((3 * LANES, width), np.float32)
    pk = np.zeros((3 * LANES, width), np.float32)
    cq = np.zeros((1, width), np.float32)
    ck = np.zeros((1, width), np.float32)
    for h in range(n_heads):
        sel[h * HEAD_DIM:(h + 1) * HEAD_DIM, h] = 1.0 / HEAD_DIM
        base = (h // 2) * LANES + (h % 2) * N_FEATURES
        for part in range(3):
            pq[part * LANES + h, base + part] = 1.0
            pk[part * LANES + h, base + 3 + part] = -1.0
            cq[0, base + 3 + part] = 1.0
            ck[0, base + part] = 1.0
    selt = (sel.T > 0).astype(np.float32)
    tri = np.tril(np.ones((tile, tile), np.float32))
    return sel, selt, tri, pq, pk, cq, ck


@functools.lru_cache(maxsize=None)
def _later_matrix(tile):
    return np.concatenate([np.triu(np.ones((tile, tile), np.float32), k=1),
                           np.ones((ONES_ROWS, tile), np.float32)], axis=0)


def kernel(x, p, ffn1_norm, ffn1_w_gate, ffn1_w_up, ffn1_w_down, mix_norm, w_in, forget_bias, q_norm, k_norm, w_branch_fox, w_branch_sb, w_out, ffn2_norm, ffn2_w_gate, ffn2_w_up, ffn2_w_down, ple_norm, w_ple_gate, w_ple_proj):
    b, s, d = x.shape
    depth = w_in.shape[0]
    fox_w = w_branch_fox.shape[1]
    sb_w = w_branch_sb.shape[1]
    n_heads = forget_bias.shape[1]
    assert fox_w == n_heads * HEAD_DIM and sb_w == fox_w and n_heads <= LANES
    assert s % TOKEN_TILE == 0 and TOKEN_TILE % ATTN_TILE == 0 and TOKEN_TILE // ATTN_TILE <= SUBLANES
    assert n_heads % HEADS_PER_STEP == 0 and 2 * N_FEATURES <= LANES
    assert (b * s) % FFN_TILE == 0 and (b * s) % POST_TILE == 0

    sel, selt, tri, pq, pk, cq, ck = _layout_constants(n_heads, TOKEN_TILE)
    later = jnp.asarray(_later_matrix(ATTN_TILE), BF16)

    xf = x.reshape(b * s, d)
    for i in range(depth):
        xf = _ffn(xf, ffn1_norm[i][None], ffn1_w_gate[i], ffn1_w_up[i], ffn1_w_down[i])

        w = jnp.swapaxes(w_in[i], 0, 1)
        o_f = 3 * fox_w
        wf = jnp.pad(w[o_f:o_f + n_heads], ((0, LANES - n_heads), (0, 0))).astype(BF16)
        fbias = jnp.pad(forget_bias[i][None], ((0, 0), (0, LANES - n_heads)))
        gq = jnp.tile(q_norm[i], n_heads)[None] * (LOG2E * HEAD_DIM ** -0.5)
        gk = jnp.tile(k_norm[i], n_heads)[None]
        consts = [mix_norm[i][None], wf, fbias, gq, gk,
                  jnp.asarray(sel, BF16), jnp.asarray(selt, BF16), jnp.asarray(tri, BF16),
                  jnp.asarray(pq, BF16), jnp.asarray(pk, BF16), jnp.asarray(cq), jnp.asarray(ck)]
        qf, kf, fvt, sq, sk, svt, gf, gs, fend = _inproj(xf.reshape(b, s, d), w, fox_w, n_heads, consts)

        fend = fend[:, :, :TOKEN_TILE // ATTN_TILE, :n_heads].reshape(b, s // ATTN_TILE, n_heads)
        fend = fend.transpose(0, 2, 1).reshape(b * n_heads, s // ATTN_TILE)
        ub = LOG2E * HEAD_DIM ** 0.5 * jnp.max(jnp.abs(q_norm[i])) * jnp.max(jnp.abs(k_norm[i]))
        margin = (2.0 * ub + ZERO_WEIGHT_BITS).reshape(1).astype(F32)

        y_fox = _fox(fend, margin, qf, kf, fvt)
        y_sb = _sb(sq, sk, svt, later)

        weights = [w_branch_fox[i], w_branch_sb[i], w_out[i], ffn2_w_gate[i], ffn2_w_up[i],
                   ffn2_w_down[i], w_ple_gate[i], w_ple_proj[i]]
        xf = _post(xf, y_fox.reshape(b * s, fox_w), y_sb.reshape(b * s, sb_w),
                   gf.reshape(b * s, d), gs.reshape(b * s, d), p[i].reshape(b * s, -1),
                   [ffn2_norm[i][None], ple_norm[i][None]], weights)
    return xf.reshape(b, s, d)
```
